```python
import math
import jax, jax.numpy as jnp
from jax import lax
import numpy as np

D_MODEL = 1024
BATCH = 4
SEQ = 4096
DEPTH = 1

D_MIX = D_MODEL
HGRN_W = D_MIX // 2
CONV_W = D_MIX - HGRN_W
HGRN_HEADS = 4
HGRN_DV = HGRN_W // HGRN_HEADS
HGRN_DK = 128
HGRN_F = HGRN_HEADS * HGRN_DK
CONV_GROUPS = 8
CONV_K = 3
CHUNK = 64
MEM_LEN = 256
MEM_HEADS = 4
MEM_HD = D_MODEL // MEM_HEADS
D_FF = int(math.ceil(8 * D_MODEL / 3 / 256) * 256)
EPS = 1e-6
IN_COLS = 3 * HGRN_F // HGRN_F * 0 + HGRN_F + HGRN_F + HGRN_W + HGRN_W + 3 * CONV_W

kernel_name = "hybrid_hgrn2_shortconv_macaron_memxattn"


def rms_norm(x, g):
    xf = x.astype(jnp.float32)
    y = xf * lax.rsqrt(jnp.mean(xf * xf, axis=-1, keepdims=True) + EPS)
    return (y * g.astype(jnp.float32)).astype(x.dtype)


def swiglu(h, w_gate, w_up, w_down):
    return (jax.nn.silu(h @ w_gate) * (h @ w_up)) @ w_down


def hgrn2_chunkwise(q, k, v, logf):
    b_, s_, h_, dk = q.shape
    dv = v.shape[-1]
    n = s_ // CHUNK

    def to_chunks(t):
        return t.astype(jnp.float32).reshape(b_, n, CHUNK, h_, t.shape[-1]).transpose(1, 0, 3, 2, 4)

    qc, kc, vc, lc = to_chunks(q), to_chunks(k), to_chunks(v), to_chunks(logf)
    bc = jnp.cumsum(lc, axis=-2)
    causal = jnp.tril(jnp.ones((CHUNK, CHUNK), dtype=bool))[None, None, :, :, None]

    def step(state, inp):
        qi, ki, vi, bi = inp
        diff = bi[:, :, :, None, :] - bi[:, :, None, :, :]
        decay = jnp.exp(jnp.where(causal, diff, -jnp.inf))
        scores = jnp.einsum('bhtk,bhsk,bhtsk->bhts', qi, ki, decay)
        o = jnp.einsum('bhts,bhsv->bhtv', scores, vi) + \
            jnp.einsum('bhtk,bhkv->bhtv', qi * jnp.exp(bi), state)
        b_last = bi[:, :, -1:, :]
        new_state = jnp.exp(b_last[:, :, 0, :])[..., None] * state + \
            jnp.einsum('bhsk,bhsv->bhkv', ki * jnp.exp(b_last - bi), vi)
        return new_state, o

    s0 = jnp.zeros((b_, h_, dk, dv), jnp.float32)
    _, o = lax.scan(step, s0, (qc, kc, vc, bc))
    return o.transpose(1, 0, 3, 2, 4).reshape(b_, s_, h_, dv)


def causal_depthwise_conv(u, w):
    s_ = u.shape[1]
    up = jnp.pad(u, ((0, 0), (CONV_K - 1, 0), (0, 0)))
    return sum(w[:, j] * up[:, j:j + s_, :] for j in range(CONV_K))


def setup_inputs(seed: int = 0) -> dict:
    key = jax.random.key(seed)
    ks = jax.random.split(key, 24)
    L = DEPTH

    def w(k, shape, fan_in):
        return jax.random.normal(k, shape, jnp.float32) * fan_in ** -0.5

    def gain(k, shape):
        return 1.0 + 0.02 * jax.random.normal(k, shape, jnp.float32)

    return {
        "x": jax.random.normal(ks[0], (BATCH, SEQ, D_MODEL), jnp.float32),
        "mem": jax.random.normal(ks[1], (BATCH, MEM_LEN, D_MODEL), jnp.float32),
        "ffn1_norm": gain(ks[2], (L, D_MODEL)),
        "ffn1_gate": w(ks[3], (L, D_MODEL, D_FF), D_MODEL),
        "ffn1_up": w(ks[4], (L, D_MODEL, D_FF), D_MODEL),
        "ffn1_down": w(ks[5], (L, D_FF, D_MODEL), D_FF),
        "mix_norm": gain(ks[6], (L, D_MODEL)),
        "w_in": w(ks[7], (L, D_MODEL, IN_COLS), D_MODEL),
        "lb_param": 0.1 * jax.random.normal(ks[8], (L + 1, HGRN_F), jnp.float32),
        "hgrn_out_norm": gain(ks[9], (L, HGRN_W)),
        "conv_w": w(ks[10], (L, CONV_W, CONV_K), CONV_K),
        "w_out": w(ks[11], (L, D_MIX, D_MODEL), D_MIX),
        "xattn_norm": gain(ks[12], (L, D_MODEL)),
        "mem_norm": gain(ks[13], (L, D_MODEL)),
        "w_q_mem": w(ks[14], (L, D_MODEL, D_MODEL), D_MODEL),
        "w_kv_mem": w(ks[15], (L, D_MODEL, 2 * D_MODEL), D_MODEL),
        "w_o_mem": w(ks[16], (L, D_MODEL, D_MODEL), D_MODEL),
        "ffn2_norm": gain(ks[17], (L, D_MODEL)),
        "ffn2_gate": w(ks[18], (L, D_MODEL, D_FF), D_MODEL),
        "ffn2_up": w(ks[19], (L, D_MODEL, D_FF), D_MODEL),
        "ffn2_down": w(ks[20], (L, D_FF, D_MODEL), D_FF),
        "final_norm": gain(ks[21], (D_MODEL,)),
    }


def reference(x, mem, ffn1_norm, ffn1_gate, ffn1_up, ffn1_down, mix_norm, w_in,
              lb_param, hgrn_out_norm, conv_w, w_out, xattn_norm, mem_norm,
              w_q_mem, w_kv_mem, w_o_mem, ffn2_norm, ffn2_gate, ffn2_up,
              ffn2_down, final_norm):
    b_, s_, _ = x.shape
    lb_all = jnp.cumsum(jax.nn.softmax(lb_param.astype(jnp.float32), axis=0), axis=0)

    for l in range(DEPTH):
        x = x + 0.5 * swiglu(rms_norm(x, ffn1_norm[l]), ffn1_gate[l], ffn1_up[l], ffn1_down[l])

        h = rms_norm(x, mix_norm[l])
        z = h @ w_in[l]
        o1 = HGRN_F; o2 = o1 + HGRN_F; o3 = o2 + HGRN_W; o4 = o3 + HGRN_W
        o5 = o4 + CONV_W; o6 = o5 + CONV_W
        zq, zf, zi, zg = z[..., :o1], z[..., o1:o2], z[..., o2:o3], z[..., o3:o4]
        zb, zc, zu = z[..., o4:o5], z[..., o5:o6], z[..., o6:]

        lb = lb_all[l]
        f = lb + (1.0 - lb) * jax.nn.sigmoid(zf.astype(jnp.float32))
        logf = jnp.log(f)
        kk = 1.0 - f
        q = jax.nn.silu(zq.astype(jnp.float32)) * HGRN_DK ** -0.5
        shp_k = (b_, s_, HGRN_HEADS, HGRN_DK)
        o_h = hgrn2_chunkwise(q.reshape(shp_k), kk.reshape(shp_k),
                              zi.reshape(b_, s_, HGRN_HEADS, HGRN_DV), logf.reshape(shp_k))
        g_h = hgrn_out_norm[l].astype(jnp.float32).reshape(HGRN_HEADS, HGRN_DV)
        o_h = o_h * lax.rsqrt(jnp.mean(o_h * o_h, axis=-1, keepdims=True) + EPS) * g_h
        y_hgrn = (o_h.reshape(b_, s_, HGRN_W) * jax.nn.silu(zg.astype(jnp.float32))).astype(x.dtype)

        y_conv = zb * causal_depthwise_conv(zc * zu, conv_w[l])

        x = x + jnp.concatenate([y_hgrn, y_conv.astype(x.dtype)], axis=-1) @ w_out[l]

        hq = rms_norm(x, xattn_norm[l])
        mn = rms_norm(mem, mem_norm[l])
        qm = (hq @ w_q_mem[l]).reshape(b_, s_, MEM_HEADS, MEM_HD)
        kv = mn @ w_kv_mem[l]
        km = kv[..., :D_MODEL].reshape(b_, MEM_LEN, MEM_HEADS, MEM_HD)
        vm = kv[..., D_MODEL:].reshape(b_, MEM_LEN, MEM_HEADS, MEM_HD)
        sc = jnp.einsum('bshd,bmhd->bhsm', qm.astype(jnp.float32), km.astype(jnp.float32)) * MEM_HD ** -0.5
        p = jax.nn.softmax(sc, axis=-1)
        att = jnp.einsum('bhsm,bmhd->bshd', p, vm.astype(jnp.float32)).astype(x.dtype)
        x = x + att.reshape(b_, s_, D_MODEL) @ w_o_mem[l]

        x = x + 0.5 * swiglu(rms_norm(x, ffn2_norm[l]), ffn2_gate[l], ffn2_up[l], ffn2_down[l])

    return rms_norm(x, final_norm)
```

```python
import functools
import math

import jax
import jax.numpy as jnp
from jax import lax
from jax.experimental import pallas as pl
from jax.experimental.pallas import tpu as pltpu

F32 = jnp.float32
BF16 = jnp.bfloat16

D_MODEL = 1024
HGRN_W = 512
CONV_W = 512
HGRN_HEADS = 4
HGRN_DK = 128
HGRN_DV = 128
HGRN_F = HGRN_HEADS * HGRN_DK
CONV_K = 3
CHUNK = 64
MEM_LEN = 256
MEM_HEADS = 4
MEM_HD = D_MODEL // MEM_HEADS
D_FF = int(math.ceil(8 * D_MODEL / 3 / 256) * 256)
EPS = 1e-6
N_IN_PIECES = 7
PIECE_W = 512

FF_CHUNK = 256
N_FF_CHUNKS = D_FF // FF_CHUNK
SUBLANES = 8
DIAG_BLOCK = SUBLANES
LEVEL_BLOCKS = (8, 16, 32)

TILE_A = 512
TILE_B = 512
TILE_C = 512
VMEM_LIMIT_BYTES = 56 * 1024 * 1024


def _dot(a, b):
    return jnp.dot(a, b, preferred_element_type=F32)


def _dot_nt(a, b):
    return lax.dot_general(a, b, (((1,), (1,)), ((), ())), preferred_element_type=F32)


def _dot_tn(a, b):
    return lax.dot_general(a, b, (((0,), (0,)), ((), ())), preferred_element_type=F32)


def _rms(x, g):
    ms = jnp.mean(x * x, axis=-1, keepdims=True)
    return x * lax.rsqrt(ms + EPS) * g


def _silu(x):
    return x * jax.nn.sigmoid(x)


def _split3(x):
    hi = x.astype(BF16)
    r1 = x - hi.astype(F32)
    mid = r1.astype(BF16)
    r2 = r1 - mid.astype(F32)
    lo = r2.astype(BF16)
    return hi, mid, lo


def _swiglu_into(h_ref, wg_ref, wu_ref, wd_ref, acc_ref):
    acc_ref[...] = jnp.zeros_like(acc_ref)

    def body(c, carry):
        h = h_ref[...]
        g = _dot(h, wg_ref[c])
        u = _dot(h, wu_ref[c])
        a = (_silu(g) * u).astype(BF16)
        acc_ref[...] += _dot(a, wd_ref[c])
        return carry

    lax.fori_loop(0, N_FF_CHUNKS, body, 0)


def _stage_a_kernel(x_ref, g1_ref, wg_ref, wu_ref, wd_ref, gm_ref, win_ref, lbp_ref, cw_ref,
                    tri_ref,
                    x1_ref, q_ref, k_ref, b_ref, v_ref, gate_ref, yc_ref,
                    h_ref, acc_ref, ubuf_ref, *, tiles_per_seq):
    i = pl.program_id(0)
    tm = x_ref.shape[0]

    x = x_ref[...]
    h_ref[...] = _rms(x, g1_ref[...]).astype(BF16)
    _swiglu_into(h_ref, wg_ref, wu_ref, wd_ref, acc_ref)
    x1 = x + 0.5 * acc_ref[...]
    x1_ref[...] = x1
    h_ref[...] = _rms(x1, gm_ref[...]).astype(BF16)

    def proj(p):
        return _dot(h_ref[...], win_ref[:, p * PIECE_W:(p + 1) * PIECE_W])

    lbp = lbp_ref[...]
    lbe = jnp.exp(lbp - jnp.max(lbp, axis=0, keepdims=True))
    lb = lbe[0:1, :] / jnp.sum(lbe, axis=0, keepdims=True)

    q_ref[...] = _silu(proj(0)) * (HGRN_DK ** -0.5)

    f = lb + (1.0 - lb) * jax.nn.sigmoid(proj(1))
    k_ref[...] = 1.0 - f
    logf = jnp.log(f)
    hi, mid, lo = _split3(logf)
    tri = tri_ref[...]
    b_ref[...] = _dot(tri, hi) + _dot(tri, mid) + _dot(tri, lo)

    v_ref[...] = proj(2)
    gate_ref[...] = _silu(proj(3))

    u = proj(5) * proj(6)

    @pl.when(i % tiles_per_seq == 0)
    def _():
        ubuf_ref[0:SUBLANES, :] = jnp.zeros((SUBLANES, CONV_W), F32)

    ubuf_ref[SUBLANES:SUBLANES + tm, :] = u
    u1 = ubuf_ref[pl.ds(SUBLANES - 1, tm), :]
    u2 = ubuf_ref[pl.ds(SUBLANES - 2, tm), :]
    cw = cw_ref[...]
    yc_ref[...] = proj(4) * (cw[0:1, :] * u2 + cw[1:2, :] * u1 + cw[2:3, :] * u)
    ubuf_ref[0:SUBLANES, :] = ubuf_ref[tm:tm + SUBLANES, :]


def _const_spec(shape):
    nd = len(shape)
    return pl.BlockSpec(shape, lambda *_: (0,) * nd, pipeline_mode=pl.Buffered(1))


def _stage_a(x2d, g1, wg, wu, wd, gm, win, lbp, cw, tri, *, seq):
    t = x2d.shape[0]
    tm = TILE_A
    row = lambda w: pl.BlockSpec((tm, w), lambda i: (i, 0))
    out_shapes = [jax.ShapeDtypeStruct((t, D_MODEL), F32)] + [
        jax.ShapeDtypeStruct((t, PIECE_W), F32) for _ in range(6)]
    return pl.pallas_call(
        functools.partial(_stage_a_kernel, tiles_per_seq=seq // tm),
        grid=(t // tm,),
        in_specs=[row(D_MODEL), _const_spec(g1.shape), _const_spec(wg.shape), _const_spec(wu.shape),
                  _const_spec(wd.shape), _const_spec(gm.shape), _const_spec(win.shape),
                  _const_spec(lbp.shape), _const_spec(cw.shape), _const_spec(tri.shape)],
        out_specs=[row(D_MODEL)] + [row(PIECE_W) for _ in range(6)],
        out_shape=out_shapes,
        scratch_shapes=[pltpu.VMEM((tm, D_MODEL), BF16), pltpu.VMEM((tm, D_MODEL), F32),
                        pltpu.VMEM((tm + SUBLANES, CONV_W), F32)],
        compiler_params=pltpu.CompilerParams(dimension_semantics=("arbitrary",),
                                             vmem_limit_bytes=VMEM_LIMIT_BYTES),
        name="stage_a_ffn1_inproj",
    )(x2d, g1, wg, wu, wd, gm, win, lbp, cw, tri)


def _intra_chunk_scores(q, k, b, masks):
    diag_mask, level_masks, odd_rows, lane_blk = masks
    nblk = CHUNK // DIAG_BLOCK
    q3 = q.reshape(nblk, DIAG_BLOCK, HGRN_DK)
    k3 = k.reshape(nblk, DIAG_BLOCK, HGRN_DK)
    b3 = b.reshape(nblk, DIAG_BLOCK, HGRN_DK)
    s3 = jnp.zeros((nblk, DIAG_BLOCK, CHUNK), F32)
    for s in range(DIAG_BLOCK):
        kb = k3[:, s:s + 1, :]
        bb = b3[:, s:s + 1, :]
        tmp = q3 * kb * jnp.exp(b3 - bb)
        red = jnp.sum(tmp, axis=-1, keepdims=True)
        s3 = jnp.where(lane_blk == s, red, s3)
    scores = jnp.where(diag_mask, s3.reshape(CHUNK, CHUNK), 0.0)
    for bs, lmask, odd in zip(LEVEL_BLOCKS, level_masks, odd_rows):
        npair = CHUNK // (2 * bs)
        r = b.reshape(npair, 2 * bs, HGRN_DK)[:, bs:bs + 1, :]
        r = jnp.broadcast_to(r, (npair, 2 * bs, HGRN_DK)).reshape(CHUNK, HGRN_DK)
        d = b - r
        e = jnp.exp(jnp.where(odd, d, -d))
        ql = jnp.where(odd, q * e, 0.0).astype(BF16)
        kl = jnp.where(odd, 0.0, k * e).astype(BF16)
        scores = jnp.where(lmask, _dot_nt(ql, kl), scores)
    return scores


def _chunk_masks():
    row = lax.broadcasted_iota(jnp.int32, (CHUNK, CHUNK), 0)
    col = lax.broadcasted_iota(jnp.int32, (CHUNK, CHUNK), 1)
    diag_mask = (row // DIAG_BLOCK == col // DIAG_BLOCK) & (col <= row)
    rk = lax.broadcasted_iota(jnp.int32, (CHUNK, HGRN_DK), 0)
    level_masks, odd_rows = [], []
    for bs in LEVEL_BLOCKS:
        level_masks.append((row // (2 * bs) == col // (2 * bs))
                           & ((row // bs) % 2 == 1) & ((col // bs) % 2 == 0))
        odd_rows.append((rk // bs) % 2 == 1)
    nblk = CHUNK // DIAG_BLOCK
    lane = lax.broadcasted_iota(jnp.int32, (nblk, DIAG_BLOCK, CHUNK), 2)
    blk = lax.broadcasted_iota(jnp.int32, (nblk, DIAG_BLOCK, CHUNK), 0)
    lane_blk = lane - blk * DIAG_BLOCK
    return diag_mask, level_masks, odd_rows, lane_blk


def _stage_b_kernel(q_ref, k_ref, b_ref, v_ref, gate_ref, gh_ref, y_ref, st_ref):
    si = pl.program_id(1)
    ts = q_ref.shape[0]

    @pl.when(si == 0)
    def _():
        st_ref[...] = jnp.zeros_like(st_ref)

    masks = _chunk_masks()

    def chunk_body(c, carry):
        r0 = pl.multiple_of(c * CHUNK, CHUNK)
        rows = pl.ds(r0, CHUNK)
        for h in range(HGRN_HEADS):
            lanes = slice(h * HGRN_DK, (h + 1) * HGRN_DK)
            vl = slice(h * HGRN_DV, (h + 1) * HGRN_DV)
            q = q_ref[rows, lanes]
            k = k_ref[rows, lanes]
            b = b_ref[rows, lanes]
            v = v_ref[rows, vl]
            vb = v.astype(BF16)
            b_last = b[CHUNK - 1:CHUNK, :]
            st = st_ref[h]
            o = _dot_nt((q * jnp.exp(b)).astype(BF16), st.astype(BF16))
            kd = (k * jnp.exp(b_last - b)).astype(BF16)
            st_ref[h] = st * jnp.exp(b_last) + _dot_tn(vb, kd)
            scores = _intra_chunk_scores(q, k, b, masks)
            o = o + _dot(scores.astype(BF16), vb)
            o = o * lax.rsqrt(jnp.mean(o * o, axis=-1, keepdims=True) + EPS) * gh_ref[:, vl]
            y_ref[rows, vl] = o * gate_ref[rows, vl]
        return carry

    lax.fori_loop(0, ts // CHUNK, chunk_body, 0)


def _stage_b(q, k, b, v, gate, gh, *, batch, seq):
    t = q.shape[0]
    ts = TILE_B
    nst = seq // ts
    row = pl.BlockSpec((ts, HGRN_F), lambda bi, si: (bi * nst + si, 0))
    return pl.pallas_call(
        _stage_b_kernel,
        grid=(batch, nst),
        in_specs=[row, row, row, row, row, pl.BlockSpec(gh.shape, lambda bi, si: (0, 0))],
        out_specs=row,
        out_shape=jax.ShapeDtypeStruct((t, HGRN_W), F32),
        scratch_shapes=[pltpu.VMEM((HGRN_HEADS, HGRN_DV, HGRN_DK), F32)],
        compiler_params=pltpu.CompilerParams(dimension_semantics=("arbitrary", "arbitrary"),
                                             vmem_limit_bytes=VMEM_LIMIT_BYTES),
        name="stage_b_hgrn2",
    )(q, k, b, v, gate, gh)


def _stage_kv_kernel(mem_ref, gn_ref, wkv_ref, k_ref, v_ref):
    mn = _rms(mem_ref[...], gn_ref[...]).astype(BF16)
    kv = _dot(mn, wkv_ref[...])
    k_ref[...] = kv[:, :D_MODEL].astype(BF16)
    v_ref[...] = kv[:, D_MODEL:].astype(BF16)


def _stage_kv(mem2d, gn, wkv, *, batch):
    blk = pl.BlockSpec((MEM_LEN, D_MODEL), lambda bi: (bi, 0))
    shp = jax.ShapeDtypeStruct((batch * MEM_LEN, D_MODEL), BF16)
    return pl.pallas_call(
        _stage_kv_kernel,
        grid=(batch,),
        in_specs=[blk, _const_spec(gn.shape), _const_spec(wkv.shape)],
        out_specs=[blk, blk],
        out_shape=[shp, shp],
        compiler_params=pltpu.CompilerParams(dimension_semantics=("arbitrary",),
                                             vmem_limit_bytes=VMEM_LIMIT_BYTES),
        name="stage_kv_mem",
    )(mem2d, gn, wkv)


def _stage_c_kernel(x1_ref, yh_ref, yc_ref, km_ref, vm_ref, wout_ref, gx_ref, wq_ref, wo_ref,
                    g2_ref, wg_ref, wu_ref, wd_ref, gf_ref,
                    out_ref,
                    h_ref, acc_ref, att_ref):
    x2 = (x1_ref[...]
          + _dot(yh_ref[...].astype(BF16), wout_ref[0:HGRN_W, :])
          + _dot(yc_ref[...].astype(BF16), wout_ref[HGRN_W:HGRN_W + CONV_W, :]))
    hq = _rms(x2, gx_ref[...]).astype(BF16)
    qm = _dot(hq, wq_ref[...]) * (MEM_HD ** -0.5)
    for h in range(MEM_HEADS):
        hs = slice(h * MEM_HD, (h + 1) * MEM_HD)
        sc = _dot_nt(qm[:, hs].astype(BF16), km_ref[:, hs])
        e = jnp.exp(sc - jnp.max(sc, axis=-1, keepdims=True))
        pv = _dot(e.astype(BF16), vm_ref[:, hs])
        att_ref[:, hs] = (pv / jnp.sum(e, axis=-1, keepdims=True)).astype(BF16)
    x3 = x2 + _dot(att_ref[...], wo_ref[...])
    h_ref[...] = _rms(x3, g2_ref[...]).astype(BF16)
    _swiglu_into(h_ref, wg_ref, wu_ref, wd_ref, acc_ref)
    x4 = x3 + 0.5 * acc_ref[...]
    out_ref[...] = _rms(x4, gf_ref[...])


def _stage_c(x1, yh, yc, km, vm, wout, gx, wq, wo, g2, wg, wu, wd, gf, *, seq):
    t = x1.shape[0]
    tm = TILE_C
    tiles_per_seq = seq // tm
    row = lambda w: pl.BlockSpec((tm, w), lambda i: (i, 0))
    memblk = pl.BlockSpec((MEM_LEN, D_MODEL), lambda i: (i // tiles_per_seq, 0))
    return pl.pallas_call(
        _stage_c_kernel,
        grid=(t // tm,),
        in_specs=[row(D_MODEL), row(HGRN_W), row(CONV_W), memblk, memblk,
                  _const_spec(wout.shape), _const_spec(gx.shape), _const_spec(wq.shape),
                  _const_spec(wo.shape), _const_spec(g2.shape), _const_spec(wg.shape),
                  _const_spec(wu.shape), _const_spec(wd.shape), _const_spec(gf.shape)],
        out_specs=row(D_MODEL),
        out_shape=jax.ShapeDtypeStruct((t, D_MODEL), F32),
        scratch_shapes=[pltpu.VMEM((tm, D_MODEL), BF16), pltpu.VMEM((tm, D_MODEL), F32),
                        pltpu.VMEM((tm, D_MODEL), BF16)],
        compiler_params=pltpu.CompilerParams(dimension_semantics=("arbitrary",),
                                             vmem_limit_bytes=VMEM_LIMIT_BYTES),
        name="stage_c_outproj_xattn_ffn2",
    )(x1, yh, yc, km, vm, wout, gx, wq, wo, g2, wg, wu, wd, gf)


def _ffn_weights(w_gate, w_up, w_down):
    wg = w_gate.astype(BF16).reshape(D_MODEL, N_FF_CHUNKS, FF_CHUNK).transpose(1, 0, 2)
    wu = w_up.astype(BF16).reshape(D_MODEL, N_FF_CHUNKS, FF_CHUNK).transpose(1, 0, 2)
    wd = w_down.astype(BF16).reshape(N_FF_CHUNKS, FF_CHUNK, D_MODEL)
    return wg, wu, wd


def _block_tri(n):
    r = lax.broadcasted_iota(jnp.int32, (n, n), 0)
    c = lax.broadcasted_iota(jnp.int32, (n, n), 1)
    return ((r // CHUNK == c // CHUNK) & (c <= r)).astype(BF16)


def kernel(x, mem, ffn1_norm, ffn1_gate, ffn1_up, ffn1_down, mix_norm, w_in, lb_param, hgrn_out_norm, conv_w, w_out, xattn_norm, mem_norm, w_q_mem, w_kv_mem, w_o_mem, ffn2_norm, ffn2_gate, ffn2_up, ffn2_down, final_norm):
    batch, seq, _ = x.shape
    depth = ffn1_norm.shape[0]
    assert depth == 1 and seq % TILE_A == 0 and seq % TILE_B == 0 and seq % TILE_C == 0
    l = 0
    x2d = x.reshape(batch * seq, D_MODEL)
    vec = lambda a: a.reshape(1, -1)

    wg1, wu1, wd1 = _ffn_weights(ffn1_gate[l], ffn1_up[l], ffn1_down[l])
    x1, q, k, b, v, gate, yc = _stage_a(
        x2d, vec(ffn1_norm[l]), wg1, wu1, wd1, vec(mix_norm[l]), w_in[l].astype(BF16),
        lb_param, conv_w[l].T, _block_tri(TILE_A), seq=seq)

    yh = _stage_b(q, k, b, v, gate, vec(hgrn_out_norm[l]), batch=batch, seq=seq)

    km, vm = _stage_kv(mem.reshape(batch * MEM_LEN, D_MODEL), vec(mem_norm[l]),
                       w_kv_mem[l].astype(BF16), batch=batch)

    wg2, wu2, wd2 = _ffn_weights(ffn2_gate[l], ffn2_up[l], ffn2_down[l])
    out = _stage_c(x1, yh, yc, km, vm, w_out[l].astype(BF16), vec(xattn_norm[l]),
                   w_q_mem[l].astype(BF16), w_o_mem[l].astype(BF16), vec(ffn2_norm[l]),
                   wg2, wu2, wd2, vec(final_norm), seq=seq)
    return out.reshape(batch, seq, D_MODEL)
```

```python
import functools
import math

import jax
import jax.numpy as jnp
from jax import lax
from jax.experimental import pallas as pl
from jax.experimental.pallas import tpu as pltpu

F32 = jnp.float32
BF16 = jnp.bfloat16

D_MODEL = 1024
HGRN_W = 512
CONV_W = 512
HGRN_HEADS = 4
HGRN_DK = 128
HGRN_DV = 128
HGRN_F = HGRN_HEADS * HGRN_DK
CONV_K = 3
CHUNK = 64
MEM_LEN = 256
MEM_HEADS = 4
MEM_HD = D_MODEL // MEM_HEADS
D_FF = int(math.ceil(8 * D_MODEL / 3 / 256) * 256)
EPS = 1e-6
N_IN_PIECES = 7
PIECE_W = 512

FF_CHUNK = 256
N_FF_CHUNKS = D_FF // FF_CHUNK
SUBLANES = 8
DIAG_BLOCK = SUBLANES
LEVEL_BLOCKS = (8, 16, 32)

TILE_A = 512
TILE_B = 512
TILE_C = 512
VMEM_LIMIT_BYTES = 56 * 1024 * 1024


def _dot(a, b):
    return jnp.dot(a, b, preferred_element_type=F32)


def _dot_nt(a, b):
    return lax.dot_general(a, b, (((1,), (1,)), ((), ())), preferred_element_type=F32)


def _dot_tn(a, b):
    return lax.dot_general(a, b, (((0,), (0,)), ((), ())), preferred_element_type=F32)


def _rms(x, g):
    ms = jnp.mean(x * x, axis=-1, keepdims=True)
    return x * lax.rsqrt(ms + EPS) * g


def _silu(x):
    return x * jax.nn.sigmoid(x)


def _split3(x):
    hi = x.astype(BF16)
    r1 = x - hi.astype(F32)
    mid = r1.astype(BF16)
    r2 = r1 - mid.astype(F32)
    lo = r2.astype(BF16)
    return hi, mid, lo


def _swiglu_into(h_ref, wg_ref, wu_ref, wd_ref, acc_ref):
    h = h_ref[...]
    for c in range(N_FF_CHUNKS):
        cols = slice(c * FF_CHUNK, (c + 1) * FF_CHUNK)
        g = _dot(h, wg_ref[:, cols])
        u = _dot(h, wu_ref[:, cols])
        a = (_silu(g) * u).astype(BF16)
        d = _dot(a, wd_ref[cols, :])
        if c == 0:
            acc_ref[...] = d
        else:
            acc_ref[...] += d


def _stage_a_kernel(x_ref, g1_ref, wg_ref, wu_ref, wd_ref, gm_ref, win_ref, lbp_ref, cw_ref,
                    tri_ref,
                    x1_ref, q_ref, k_ref, b_ref, v_ref, gate_ref, yc_ref,
                    h_ref, acc_ref, ubuf_ref, *, tiles_per_seq):
    i = pl.program_id(0)
    tm = x_ref.shape[0]

    x = x_ref[...]
    h_ref[...] = _rms(x, g1_ref[...]).astype(BF16)
    _swiglu_into(h_ref, wg_ref, wu_ref, wd_ref, acc_ref)
    x1 = x + 0.5 * acc_ref[...]
    x1_ref[...] = x1
    h_ref[...] = _rms(x1, gm_ref[...]).astype(BF16)

    def proj(p):
        return _dot(h_ref[...], win_ref[:, p * PIECE_W:(p + 1) * PIECE_W])

    lbp = lbp_ref[...]
    lbe = jnp.exp(lbp - jnp.max(lbp, axis=0, keepdims=True))
    lb = lbe[0:1, :] / jnp.sum(lbe, axis=0, keepdims=True)

    q_ref[...] = _silu(proj(0)) * (HGRN_DK ** -0.5)

    f = lb + (1.0 - lb) * jax.nn.sigmoid(proj(1))
    k_ref[...] = 1.0 - f
    logf = jnp.log(f)
    hi, mid, lo = _split3(logf)
    tri = tri_ref[...]
    b_ref[...] = _dot(tri, hi) + _dot(tri, mid) + _dot(tri, lo)

    v_ref[...] = proj(2)
    gate_ref[...] = _silu(proj(3))

    u = proj(5) * proj(6)

    @pl.when(i % tiles_per_seq == 0)
    def _():
        ubuf_ref[0:SUBLANES, :] = jnp.zeros((SUBLANES, CONV_W), F32)

    ubuf_ref[SUBLANES:SUBLANES + tm, :] = u
    u1 = ubuf_ref[pl.ds(SUBLANES - 1, tm), :]
    u2 = ubuf_ref[pl.ds(SUBLANES - 2, tm), :]
    cw = cw_ref[...]
    yc_ref[...] = proj(4) * (cw[0:1, :] * u2 + cw[1:2, :] * u1 + cw[2:3, :] * u)
    ubuf_ref[0:SUBLANES, :] = ubuf_ref[tm:tm + SUBLANES, :]


def _const_spec(shape):
    nd = len(shape)
    return pl.BlockSpec(shape, lambda *_: (0,) * nd, pipeline_mode=pl.Buffered(1))


def _stage_a(x2d, g1, wg, wu, wd, gm, win, lbp, cw, tri, *, seq):
    t = x2d.shape[0]
    tm = TILE_A
    row = lambda w: pl.BlockSpec((tm, w), lambda i: (i, 0))
    out_shapes = [jax.ShapeDtypeStruct((t, D_MODEL), F32)] + [
        jax.ShapeDtypeStruct((t, PIECE_W), F32) for _ in range(6)]
    return pl.pallas_call(
        functools.partial(_stage_a_kernel, tiles_per_seq=seq // tm),
        grid=(t // tm,),
        in_specs=[row(D_MODEL), _const_spec(g1.shape), _const_spec(wg.shape), _const_spec(wu.shape),
                  _const_spec(wd.shape), _const_spec(gm.shape), _const_spec(win.shape),
                  _const_spec(lbp.shape), _const_spec(cw.shape), _const_spec(tri.shape)],
        out_specs=[row(D_MODEL)] + [row(PIECE_W) for _ in range(6)],
        out_shape=out_shapes,
        scratch_shapes=[pltpu.VMEM((tm, D_MODEL), BF16), pltpu.VMEM((tm, D_MODEL), F32),
                        pltpu.VMEM((tm + SUBLANES, CONV_W), F32)],
        compiler_params=pltpu.CompilerParams(dimension_semantics=("arbitrary",),
                                             vmem_limit_bytes=VMEM_LIMIT_BYTES),
        name="stage_a_ffn1_inproj",
    )(x2d, g1, wg, wu, wd, gm, win, lbp, cw, tri)


def _intra_chunk_scores(q, k, b, masks):
    diag_mask, level_masks, odd_rows, lane_blk = masks
    nblk = CHUNK // DIAG_BLOCK
    q3 = q.reshape(nblk, DIAG_BLOCK, HGRN_DK)
    k3 = k.reshape(nblk, DIAG_BLOCK, HGRN_DK)
    b3 = b.reshape(nblk, DIAG_BLOCK, HGRN_DK)
    s3 = jnp.zeros((nblk, DIAG_BLOCK, CHUNK), F32)
    for s in range(DIAG_BLOCK):
        kb = k3[:, s:s + 1, :]
        bb = b3[:, s:s + 1, :]
        tmp = q3 * kb * jnp.exp(b3 - bb)
        red = jnp.sum(tmp, axis=-1, keepdims=True)
        s3 = jnp.where(lane_blk == s, red, s3)
    scores = jnp.where(diag_mask, s3.reshape(CHUNK, CHUNK), 0.0)
    for bs, lmask, odd in zip(LEVEL_BLOCKS, level_masks, odd_rows):
        npair = CHUNK // (2 * bs)
        r = b.reshape(npair, 2 * bs, HGRN_DK)[:, bs:bs + 1, :]
        r = jnp.broadcast_to(r, (npair, 2 * bs, HGRN_DK)).reshape(CHUNK, HGRN_DK)
        d = b - r
        e = jnp.exp(jnp.where(odd, d, -d))
        ql = jnp.where(odd, q * e, 0.0).astype(BF16)
        kl = jnp.where(odd, 0.0, k * e).astype(BF16)
        scores = jnp.where(lmask, _dot_nt(ql, kl), scores)
    return scores


def _chunk_masks():
    row = lax.broadcasted_iota(jnp.int32, (CHUNK, CHUNK), 0)
    col = lax.broadcasted_iota(jnp.int32, (CHUNK, CHUNK), 1)
    diag_mask = (row // DIAG_BLOCK == col // DIAG_BLOCK) & (col <= row)
    rk = lax.broadcasted_iota(jnp.int32, (CHUNK, HGRN_DK), 0)
    level_masks, odd_rows = [], []
    for bs in LEVEL_BLOCKS:
        level_masks.append((row // (2 * bs) == col // (2 * bs))
                           & ((row // bs) % 2 == 1) & ((col // bs) % 2 == 0))
        odd_rows.append((rk // bs) % 2 == 1)
    nblk = CHUNK // DIAG_BLOCK
    lane = lax.broadcasted_iota(jnp.int32, (nblk, DIAG_BLOCK, CHUNK), 2)
    blk = lax.broadcasted_iota(jnp.int32, (nblk, DIAG_BLOCK, CHUNK), 0)
    lane_blk = lane - blk * DIAG_BLOCK
    return diag_mask, level_masks, odd_rows, lane_blk


def _stage_b_kernel(q_ref, k_ref, b_ref, v_ref, gate_ref, gh_ref, y_ref, st_ref):
    si = pl.program_id(1)
    ts = q_ref.shape[0]

    @pl.when(si == 0)
    def _():
        st_ref[...] = jnp.zeros_like(st_ref)

    masks = _chunk_masks()

    def chunk_body(c, carry):
        r0 = pl.multiple_of(c * CHUNK, CHUNK)
        rows = pl.ds(r0, CHUNK)
        for h in range(HGRN_HEADS):
            lanes = slice(h * HGRN_DK, (h + 1) * HGRN_DK)
            vl = slice(h * HGRN_DV, (h + 1) * HGRN_DV)
            q = q_ref[rows, lanes]
            k = k_ref[rows, lanes]
            b = b_ref[rows, lanes]
            v = v_ref[rows, vl]
            vb = v.astype(BF16)
            b_last = b[CHUNK - 1:CHUNK, :]
            st = st_ref[h]
            o = _dot_nt((q * jnp.exp(b)).astype(BF16), st.astype(BF16))
            kd = (k * jnp.exp(b_last - b)).astype(BF16)
            st_ref[h] = st * jnp.exp(b_last) + _dot_tn(vb, kd)
            scores = _intra_chunk_scores(q, k, b, masks)
            o = o + _dot(scores.astype(BF16), vb)
            o = o * lax.rsqrt(jnp.mean(o * o, axis=-1, keepdims=True) + EPS) * gh_ref[:, vl]
            y_ref[rows, vl] = o * gate_ref[rows, vl]
        return carry

    lax.fori_loop(0, ts // CHUNK, chunk_body, 0, unroll=2)


def _stage_b(q, k, b, v, gate, gh, *, batch, seq):
    t = q.shape[0]
    ts = TILE_B
    nst = seq // ts
    row = pl.BlockSpec((ts, HGRN_F), lambda bi, si: (bi * nst + si, 0))
    return pl.pallas_call(
        _stage_b_kernel,
        grid=(batch, nst),
        in_specs=[row, row, row, row, row, pl.BlockSpec(gh.shape, lambda bi, si: (0, 0))],
        out_specs=row,
        out_shape=jax.ShapeDtypeStruct((t, HGRN_W), F32),
        scratch_shapes=[pltpu.VMEM((HGRN_HEADS, HGRN_DV, HGRN_DK), F32)],
        compiler_params=pltpu.CompilerParams(dimension_semantics=("arbitrary", "arbitrary"),
                                             vmem_limit_bytes=VMEM_LIMIT_BYTES),
        name="stage_b_hgrn2",
    )(q, k, b, v, gate, gh)


def _stage_kv_kernel(mem_ref, gn_ref, wkv_ref, k_ref, v_ref):
    mn = _rms(mem_ref[...], gn_ref[...]).astype(BF16)
    kv = _dot(mn, wkv_ref[...])
    k_ref[...] = kv[:, :D_MODEL].astype(BF16)
    v_ref[...] = kv[:, D_MODEL:].astype(BF16)


def _stage_kv(mem2d, gn, wkv, *, batch):
    blk = pl.BlockSpec((MEM_LEN, D_MODEL), lambda bi: (bi, 0))
    shp = jax.ShapeDtypeStruct((batch * MEM_LEN, D_MODEL), BF16)
    return pl.pallas_call(
        _stage_kv_kernel,
        grid=(batch,),
        in_specs=[blk, _const_spec(gn.shape), _const_spec(wkv.shape)],
        out_specs=[blk, blk],
        out_shape=[shp, shp],
        compiler_params=pltpu.CompilerParams(dimension_semantics=("arbitrary",),
                                             vmem_limit_bytes=VMEM_LIMIT_BYTES),
        name="stage_kv_mem",
    )(mem2d, gn, wkv)


def _stage_c_kernel(x1_ref, yh_ref, yc_ref, km_ref, vm_ref, wout_ref, gx_ref, wq_ref, wo_ref,
                    g2_ref, wg_ref, wu_ref, wd_ref, gf_ref,
                    out_ref,
                    h_ref, acc_ref, att_ref):
    x2 = (x1_ref[...]
          + _dot(yh_ref[...].astype(BF16), wout_ref[0:HGRN_W, :])
          + _dot(yc_ref[...].astype(BF16), wout_ref[HGRN_W:HGRN_W + CONV_W, :]))
    hq = _rms(x2, gx_ref[...]).astype(BF16)
    qm = _dot(hq, wq_ref[...]) * (MEM_HD ** -0.5)
    for h in range(MEM_HEADS):
        hs = slice(h * MEM_HD, (h + 1) * MEM_HD)
        sc = _dot_nt(qm[:, hs].astype(BF16), km_ref[:, hs])
        e = jnp.exp(sc - jnp.max(sc, axis=-1, keepdims=True))
        pv = _dot(e.astype(BF16), vm_ref[:, hs])
        att_ref[:, hs] = (pv / jnp.sum(e, axis=-1, keepdims=True)).astype(BF16)
    x3 = x2 + _dot(att_ref[...], wo_ref[...])
    h_ref[...] = _rms(x3, g2_ref[...]).astype(BF16)
    _swiglu_into(h_ref, wg_ref, wu_ref, wd_ref, acc_ref)
    x4 = x3 + 0.5 * acc_ref[...]
    out_ref[...] = _rms(x4, gf_ref[...])


def _stage_c(x1, yh, yc, km, vm, wout, gx, wq, wo, g2, wg, wu, wd, gf, *, seq):
    t = x1.shape[0]
    tm = TILE_C
    tiles_per_seq = seq // tm
    row = lambda w: pl.BlockSpec((tm, w), lambda i: (i, 0))
    memblk = pl.BlockSpec((MEM_LEN, D_MODEL), lambda i: (i // tiles_per_seq, 0))
    return pl.pallas_call(
        _stage_c_kernel,
        grid=(t // tm,),
        in_specs=[row(D_MODEL), row(HGRN_W), row(CONV_W), memblk, memblk,
                  _const_spec(wout.shape), _const_spec(gx.shape), _const_spec(wq.shape),
                  _const_spec(wo.shape), _const_spec(g2.shape), _const_spec(wg.shape),
                  _const_spec(wu.shape), _const_spec(wd.shape), _const_spec(gf.shape)],
        out_specs=row(D_MODEL),
        out_shape=jax.ShapeDtypeStruct((t, D_MODEL), F32),
        scratch_shapes=[pltpu.VMEM((tm, D_MODEL), BF16), pltpu.VMEM((tm, D_MODEL), F32),
                        pltpu.VMEM((tm, D_MODEL), BF16)],
        compiler_params=pltpu.CompilerParams(dimension_semantics=("arbitrary",),
                                             vmem_limit_bytes=VMEM_LIMIT_BYTES),
        name="stage_c_outproj_xattn_ffn2",
    )(x1, yh, yc, km, vm, wout, gx, wq, wo, g2, wg, wu, wd, gf)


def _ffn_weights(w_gate, w_up, w_down):
    return w_gate.astype(BF16), w_up.astype(BF16), w_down.astype(BF16)


def _block_tri(n):
    r = lax.broadcasted_iota(jnp.int32, (n, n), 0)
    c = lax.broadcasted_iota(jnp.int32, (n, n), 1)
    return ((r // CHUNK == c // CHUNK) & (c <= r)).astype(BF16)


def kernel(x, mem, ffn1_norm, ffn1_gate, ffn1_up, ffn1_down, mix_norm, w_in, lb_param, hgrn_out_norm, conv_w, w_out, xattn_norm, mem_norm, w_q_mem, w_kv_mem, w_o_mem, ffn2_norm, ffn2_gate, ffn2_up, ffn2_down, final_norm):
    batch, seq, _ = x.shape
    depth = ffn1_norm.shape[0]
    assert depth == 1 and seq % TILE_A == 0 and seq % TILE_B == 0 and seq % TILE_C == 0
    l = 0
    x2d = x.reshape(batch * seq, D_MODEL)
    vec = lambda a: a.reshape(1, -1)

    wg1, wu1, wd1 = _ffn_weights(ffn1_gate[l], ffn1_up[l], ffn1_down[l])
    x1, q, k, b, v, gate, yc = _stage_a(
        x2d, vec(ffn1_norm[l]), wg1, wu1, wd1, vec(mix_norm[l]), w_in[l].astype(BF16),
        lb_param, conv_w[l].T, _block_tri(TILE_A), seq=seq)

    yh = _stage_b(q, k, b, v, gate, vec(hgrn_out_norm[l]), batch=batch, seq=seq)

    km, vm = _stage_kv(mem.reshape(batch * MEM_LEN, D_MODEL), vec(mem_norm[l]),
                       w_kv_mem[l].astype(BF16), batch=batch)

    wg2, wu2, wd2 = _ffn_weights(ffn2_gate[l], ffn2_up[l], ffn2_down[l])
    out = _stage_c(x1, yh, yc, km, vm, w_out[l].astype(BF16), vec(xattn_norm[l]),
                   w_q_mem[l].astype(BF16), w_o_mem[l].astype(BF16), vec(ffn2_norm[l]),
                   wg2, wu2, wd2, vec(final_norm), seq=seq)
    return out.reshape(batch, seq, D_MODEL)
```

```python
import functools
import math

import jax
import jax.numpy as jnp
from jax import lax
from jax.experimental import pallas as pl
from jax.experimental.pallas import tpu as pltpu

F32 = jnp.float32
BF16 = jnp.bfloat16

D_MODEL = 1024
HGRN_W = 512
CONV_W = 512
HGRN_HEADS = 4
HGRN_DK = 128
HGRN_DV = 128
HGRN_F = HGRN_HEADS * HGRN_DK
CONV_K = 3
CHUNK = 64
MEM_LEN = 256
MEM_HEADS = 4
MEM_HD = D_MODEL // MEM_HEADS
D_FF = int(math.ceil(8 * D_MODEL / 3 / 256) * 256)
EPS = 1e-6
N_IN_PIECES = 7
PIECE_W = 512

FF_CHUNK = 256
N_FF_CHUNKS = D_FF // FF_CHUNK
SUBLANES = 8
DIAG_BLOCK = SUBLANES
LEVEL_BLOCKS = (8, 16, 32)

TILE_A = 512
TILE_C = 512
VMEM_LIMIT_BYTES = 56 * 1024 * 1024


def _dot(a, b):
    return jnp.dot(a, b, preferred_element_type=F32)


def _dot_nt(a, b):
    return lax.dot_general(a, b, (((1,), (1,)), ((), ())), preferred_element_type=F32)


def _dot_tn(a, b):
    return lax.dot_general(a, b, (((0,), (0,)), ((), ())), preferred_element_type=F32)


def _rms(x, g):
    ms = jnp.mean(x * x, axis=-1, keepdims=True)
    return x * lax.rsqrt(ms + EPS) * g


def _silu(x):
    return x * jax.nn.sigmoid(x)


def _split3(x):
    hi = x.astype(BF16)
    r1 = x - hi.astype(F32)
    mid = r1.astype(BF16)
    r2 = r1 - mid.astype(F32)
    lo = r2.astype(BF16)
    return hi, mid, lo


def _swiglu_into(h_ref, wg_ref, wu_ref, wd_ref, acc_ref, side_work=()):
    h = h_ref[...]
    per_chunk = -(-len(side_work) // N_FF_CHUNKS)
    for c in range(N_FF_CHUNKS):
        cols = slice(c * FF_CHUNK, (c + 1) * FF_CHUNK)
        g = _dot(h, wg_ref[:, cols])
        u = _dot(h, wu_ref[:, cols])
        for thunk in side_work[c * per_chunk:(c + 1) * per_chunk]:
            thunk()
        a = (_silu(g) * u).astype(BF16)
        d = _dot(a, wd_ref[cols, :])
        if c == 0:
            acc_ref[...] = d
        else:
            acc_ref[...] += d


def _intra_chunk_scores(q, k, b, masks):
    diag_mask, level_masks, odd_rows, lane_blk = masks
    nblk = CHUNK // DIAG_BLOCK
    q3 = q.reshape(nblk, DIAG_BLOCK, HGRN_DK)
    k3 = k.reshape(nblk, DIAG_BLOCK, HGRN_DK)
    b3 = b.reshape(nblk, DIAG_BLOCK, HGRN_DK)
    s3 = jnp.zeros((nblk, DIAG_BLOCK, CHUNK), F32)
    for s in range(DIAG_BLOCK):
        kb = k3[:, s:s + 1, :]
        bb = b3[:, s:s + 1, :]
        tmp = q3 * kb * jnp.exp(b3 - bb)
        red = jnp.sum(tmp, axis=-1, keepdims=True)
        s3 = jnp.where(lane_blk == s, red, s3)
    scores = jnp.where(diag_mask, s3.reshape(CHUNK, CHUNK), 0.0)
    for bs, lmask, odd in zip(LEVEL_BLOCKS, level_masks, odd_rows):
        npair = CHUNK // (2 * bs)
        r = b.reshape(npair, 2 * bs, HGRN_DK)[:, bs:bs + 1, :]
        r = jnp.broadcast_to(r, (npair, 2 * bs, HGRN_DK)).reshape(CHUNK, HGRN_DK)
        d = b - r
        e = jnp.exp(jnp.where(odd, d, -d))
        ql = jnp.where(odd, q * e, 0.0).astype(BF16)
        kl = jnp.where(odd, 0.0, k * e).astype(BF16)
        scores = jnp.where(lmask, _dot_nt(ql, kl), scores)
    return scores


def _chunk_masks():
    row = lax.broadcasted_iota(jnp.int32, (CHUNK, CHUNK), 0)
    col = lax.broadcasted_iota(jnp.int32, (CHUNK, CHUNK), 1)
    diag_mask = (row // DIAG_BLOCK == col // DIAG_BLOCK) & (col <= row)
    rk = lax.broadcasted_iota(jnp.int32, (CHUNK, HGRN_DK), 0)
    level_masks, odd_rows = [], []
    for bs in LEVEL_BLOCKS:
        level_masks.append((row // (2 * bs) == col // (2 * bs))
                           & ((row // bs) % 2 == 1) & ((col // bs) % 2 == 0))
        odd_rows.append((rk // bs) % 2 == 1)
    nblk = CHUNK // DIAG_BLOCK
    lane = lax.broadcasted_iota(jnp.int32, (nblk, DIAG_BLOCK, CHUNK), 2)
    blk = lax.broadcasted_iota(jnp.int32, (nblk, DIAG_BLOCK, CHUNK), 0)
    lane_blk = lane - blk * DIAG_BLOCK
    return diag_mask, level_masks, odd_rows, lane_blk


def _hgrn_units(slot, q_s, k_s, b_s, v_s, g_s, gh_ref, y_ref, st_ref):
    masks = _chunk_masks()

    def unit(c, h):
        rows = slice(c * CHUNK, (c + 1) * CHUNK)
        lanes = slice(h * HGRN_DK, (h + 1) * HGRN_DK)
        vl = slice(h * HGRN_DV, (h + 1) * HGRN_DV)
        q = q_s[slot, rows, lanes]
        k = k_s[slot, rows, lanes]
        b = b_s[slot, rows, lanes]
        vb = v_s[slot, rows, vl].astype(BF16)
        b_last = b[CHUNK - 1:CHUNK, :]
        st = st_ref[h]
        o = _dot_nt((q * jnp.exp(b)).astype(BF16), st.astype(BF16))
        kd = (k * jnp.exp(b_last - b)).astype(BF16)
        st_ref[h] = st * jnp.exp(b_last) + _dot_tn(vb, kd)
        scores = _intra_chunk_scores(q, k, b, masks)
        o = o + _dot(scores.astype(BF16), vb)
        o = o * lax.rsqrt(jnp.mean(o * o, axis=-1, keepdims=True) + EPS) * gh_ref[:, vl]
        y_ref[rows, vl] = o * g_s[slot, rows, vl]

    return [functools.partial(unit, c, h)
            for c in range(q_s.shape[1] // CHUNK) for h in range(HGRN_HEADS)]


def _inproj_tile(slot, side_work, x_ref, g1_ref, wg_ref, wu_ref, wd_ref, gm_ref, win_ref, lbp_ref,
                 cw_ref, tri_ref, x1_ref, yc_ref, h_ref, acc_ref, ubuf_ref, q_s, k_s, b_s, v_s, g_s):
    tm = x_ref.shape[0]
    x = x_ref[...]
    h_ref[...] = _rms(x, g1_ref[...]).astype(BF16)
    _swiglu_into(h_ref, wg_ref, wu_ref, wd_ref, acc_ref, side_work)
    x1 = x + 0.5 * acc_ref[...]
    x1_ref[...] = x1
    h_ref[...] = _rms(x1, gm_ref[...]).astype(BF16)

    def proj(p):
        return _dot(h_ref[...], win_ref[:, p * PIECE_W:(p + 1) * PIECE_W])

    lbp = lbp_ref[...]
    lbe = jnp.exp(lbp - jnp.max(lbp, axis=0, keepdims=True))
    lb = lbe[0:1, :] / jnp.sum(lbe, axis=0, keepdims=True)

    q_s[slot] = _silu(proj(0)) * (HGRN_DK ** -0.5)

    f = lb + (1.0 - lb) * jax.nn.sigmoid(proj(1))
    k_s[slot] = 1.0 - f
    logf = jnp.log(f)
    hi, mid, lo = _split3(logf)
    tri = tri_ref[...]
    b_s[slot] = _dot(tri, hi) + _dot(tri, mid) + _dot(tri, lo)

    v_s[slot] = proj(2)
    g_s[slot] = _silu(proj(3))

    u = proj(5) * proj(6)
    ubuf_ref[SUBLANES:SUBLANES + tm, :] = u
    u1 = ubuf_ref[pl.ds(SUBLANES - 1, tm), :]
    u2 = ubuf_ref[pl.ds(SUBLANES - 2, tm), :]
    cw = cw_ref[...]
    yc_ref[...] = proj(4) * (cw[0:1, :] * u2 + cw[1:2, :] * u1 + cw[2:3, :] * u)
    ubuf_ref[0:SUBLANES, :] = ubuf_ref[tm:tm + SUBLANES, :]


def _stage_ab_kernel(x_ref, g1_ref, wg_ref, wu_ref, wd_ref, gm_ref, win_ref, lbp_ref, cw_ref,
                     tri_ref, gh_ref,
                     x1_ref, yh_ref, yc_ref,
                     h_ref, acc_ref, ubuf_ref, q_s, k_s, b_s, v_s, g_s, st_ref,
                     *, tiles_per_seq, n_tiles):
    i = pl.program_id(0)
    wslot = i % 2
    rslot = 1 - wslot
    staging = (q_s, k_s, b_s, v_s, g_s)

    @pl.when(i == 0)
    def _():
        for ref in staging:
            ref[rslot] = jnp.zeros(ref.shape[1:], F32)

    @pl.when((i + tiles_per_seq - 1) % tiles_per_seq == 0)
    def _():
        st_ref[...] = jnp.zeros_like(st_ref)

    @pl.when(i % tiles_per_seq == 0)
    def _():
        ubuf_ref[0:SUBLANES, :] = jnp.zeros((SUBLANES, CONV_W), F32)

    @pl.when(i < n_tiles)
    def _():
        hgrn = _hgrn_units(rslot, *staging, gh_ref, yh_ref, st_ref)
        _inproj_tile(wslot, hgrn, x_ref, g1_ref, wg_ref, wu_ref, wd_ref, gm_ref, win_ref, lbp_ref,
                     cw_ref, tri_ref, x1_ref, yc_ref, h_ref, acc_ref, ubuf_ref, *staging)

    @pl.when(i == n_tiles)
    def _():
        for thunk in _hgrn_units(rslot, *staging, gh_ref, yh_ref, st_ref):
            thunk()


def _const_spec(shape):
    nd = len(shape)
    return pl.BlockSpec(shape, lambda *_: (0,) * nd, pipeline_mode=pl.Buffered(1))


def _stage_ab(x2d, g1, wg, wu, wd, gm, win, lbp, cw, tri, gh, *, seq):
    t = x2d.shape[0]
    tm = TILE_A
    n_tiles = t // tm
    cur = lambda w: pl.BlockSpec((tm, w), lambda i: (jnp.minimum(i, n_tiles - 1), 0))
    prev = lambda w: pl.BlockSpec((tm, w), lambda i: (jnp.maximum(i - 1, 0), 0))
    consts = (g1, wg, wu, wd, gm, win, lbp, cw, tri, gh)
    stage = pltpu.VMEM((2, tm, PIECE_W), F32)
    return pl.pallas_call(
        functools.partial(_stage_ab_kernel, tiles_per_seq=seq // tm, n_tiles=n_tiles),
        grid=(n_tiles + 1,),
        in_specs=[cur(D_MODEL)] + [_const_spec(c.shape) for c in consts],
        out_specs=[cur(D_MODEL), prev(HGRN_W), cur(CONV_W)],
        out_shape=[jax.ShapeDtypeStruct((t, D_MODEL), F32), jax.ShapeDtypeStruct((t, HGRN_W), F32),
                   jax.ShapeDtypeStruct((t, CONV_W), F32)],
        scratch_shapes=[pltpu.VMEM((tm, D_MODEL), BF16), pltpu.VMEM((tm, D_MODEL), F32),
                        pltpu.VMEM((tm + SUBLANES, CONV_W), F32),
                        stage, stage, stage, stage, stage,
                        pltpu.VMEM((HGRN_HEADS, HGRN_DV, HGRN_DK), F32)],
        compiler_params=pltpu.CompilerParams(dimension_semantics=("arbitrary",),
                                             vmem_limit_bytes=VMEM_LIMIT_BYTES),
        name="stage_ab_ffn1_inproj_hgrn2",
    )(x2d, *consts)


def _stage_kv_kernel(mem_ref, gn_ref, wkv_ref, k_ref, v_ref):
    mn = _rms(mem_ref[...], gn_ref[...]).astype(BF16)
    kv = _dot(mn, wkv_ref[...])
    k_ref[...] = kv[:, :D_MODEL].astype(BF16)
    v_ref[...] = kv[:, D_MODEL:].astype(BF16)


def _stage_kv(mem2d, gn, wkv, *, batch):
    blk = pl.BlockSpec((MEM_LEN, D_MODEL), lambda bi: (bi, 0))
    shp = jax.ShapeDtypeStruct((batch * MEM_LEN, D_MODEL), BF16)
    return pl.pallas_call(
        _stage_kv_kernel,
        grid=(batch,),
        in_specs=[blk, _const_spec(gn.shape), _const_spec(wkv.shape)],
        out_specs=[blk, blk],
        out_shape=[shp, shp],
        compiler_params=pltpu.CompilerParams(dimension_semantics=("arbitrary",),
                                             vmem_limit_bytes=VMEM_LIMIT_BYTES),
        name="stage_kv_mem",
    )(mem2d, gn, wkv)


def _stage_c_kernel(x1_ref, yh_ref, yc_ref, km_ref, vm_ref, wout_ref, gx_ref, wq_ref, wo_ref,
                    g2_ref, wg_ref, wu_ref, wd_ref, gf_ref,
                    out_ref,
                    h_ref, acc_ref, att_ref):
    x2 = (x1_ref[...]
          + _dot(yh_ref[...].astype(BF16), wout_ref[0:HGRN_W, :])
          + _dot(yc_ref[...].astype(BF16), wout_ref[HGRN_W:HGRN_W + CONV_W, :]))
    hq = _rms(x2, gx_ref[...]).astype(BF16)
    qm = _dot(hq, wq_ref[...]) * (MEM_HD ** -0.5)
    for h in range(MEM_HEADS):
        hs = slice(h * MEM_HD, (h + 1) * MEM_HD)
        sc = _dot_nt(qm[:, hs].astype(BF16), km_ref[:, hs])
        e = jnp.exp(sc - jnp.max(sc, axis=-1, keepdims=True))
        pv = _dot(e.astype(BF16), vm_ref[:, hs])
        att_ref[:, hs] = (pv / jnp.sum(e, axis=-1, keepdims=True)).astype(BF16)
    x3 = x2 + _dot(att_ref[...], wo_ref[...])
    h_ref[...] = _rms(x3, g2_ref[...]).astype(BF16)
    _swiglu_into(h_ref, wg_ref, wu_ref, wd_ref, acc_ref)
    x4 = x3 + 0.5 * acc_ref[...]
    out_ref[...] = _rms(x4, gf_ref[...])


def _stage_c(x1, yh, yc, km, vm, wout, gx, wq, wo, g2, wg, wu, wd, gf, *, seq):
    t = x1.shape[0]
    tm = TILE_C
    tiles_per_seq = seq // tm
    row = lambda w: pl.BlockSpec((tm, w), lambda i: (i, 0))
    memblk = pl.BlockSpec((MEM_LEN, D_MODEL), lambda i: (i // tiles_per_seq, 0))
    return pl.pallas_call(
        _stage_c_kernel,
        grid=(t // tm,),
        in_specs=[row(D_MODEL), row(HGRN_W), row(CONV_W), memblk, memblk,
                  _const_spec(wout.shape), _const_spec(gx.shape), _const_spec(wq.shape),
                  _const_spec(wo.shape), _const_spec(g2.shape), _const_spec(wg.shape),
                  _const_spec(wu.shape), _const_spec(wd.shape), _const_spec(gf.shape)],
        out_specs=row(D_MODEL),
        out_shape=jax.ShapeDtypeStruct((t, D_MODEL), F32),
        scratch_shapes=[pltpu.VMEM((tm, D_MODEL), BF16), pltpu.VMEM((tm, D_MODEL), F32),
                        pltpu.VMEM((tm, D_MODEL), BF16)],
        compiler_params=pltpu.CompilerParams(dimension_semantics=("arbitrary",),
                                             vmem_limit_bytes=VMEM_LIMIT_BYTES),
        name="stage_c_outproj_xattn_ffn2",
    )(x1, yh, yc, km, vm, wout, gx, wq, wo, g2, wg, wu, wd, gf)


def _ffn_weights(w_gate, w_up, w_down):
    return w_gate.astype(BF16), w_up.astype(BF16), w_down.astype(BF16)


def _block_tri(n):
    r = lax.broadcasted_iota(jnp.int32, (n, n), 0)
    c = lax.broadcasted_iota(jnp.int32, (n, n), 1)
    return ((r // CHUNK == c // CHUNK) & (c <= r)).astype(BF16)


def kernel(x, mem, ffn1_norm, ffn1_gate, ffn1_up, ffn1_down, mix_norm, w_in, lb_param, hgrn_out_norm, conv_w, w_out, xattn_norm, mem_norm, w_q_mem, w_kv_mem, w_o_mem, ffn2_norm, ffn2_gate, ffn2_up, ffn2_down, final_norm):
    batch, seq, _ = x.shape
    depth = ffn1_norm.shape[0]
    assert depth == 1 and seq % TILE_A == 0 and seq % TILE_C == 0
    l = 0
    x2d = x.reshape(batch * seq, D_MODEL)
    vec = lambda a: a.reshape(1, -1)

    wg1, wu1, wd1 = _ffn_weights(ffn1_gate[l], ffn1_up[l], ffn1_down[l])
    x1, yh, yc = _stage_ab(
        x2d, vec(ffn1_norm[l]), wg1, wu1, wd1, vec(mix_norm[l]), w_in[l].astype(BF16),
        lb_param, conv_w[l].T, _block_tri(TILE_A), vec(hgrn_out_norm[l]), seq=seq)

    km, vm = _stage_kv(mem.reshape(batch * MEM_LEN, D_MODEL), vec(mem_norm[l]),
                       w_kv_mem[l].astype(BF16), batch=batch)

    wg2, wu2, wd2 = _ffn_weights(ffn2_gate[l], ffn2_up[l], ffn2_down[l])
    out = _stage_c(x1, yh, yc, km, vm, w_out[l].astype(BF16), vec(xattn_norm[l]),
                   w_q_mem[l].astype(BF16), w_o_mem[l].astype(BF16), vec(ffn2_norm[l]),
                   wg2, wu2, wd2, vec(final_norm), seq=seq)
    return out.reshape(batch, seq, D_MODEL)
```

```python
import functools
import math

import jax
import jax.numpy as jnp
from jax import lax
from jax.experimental import pallas as pl
from jax.experimental.pallas import tpu as pltpu

F32 = jnp.float32
BF16 = jnp.bfloat16

D_MODEL = 1024
HGRN_W = 512
CONV_W = 512
HGRN_HEADS = 4
HGRN_DK = 128
HGRN_DV = 128
HGRN_F = HGRN_HEADS * HGRN_DK
CONV_K = 3
CHUNK = 64
MEM_LEN = 256
MEM_HEADS = 4
MEM_HD = D_MODEL // MEM_HEADS
D_FF = int(math.ceil(8 * D_MODEL / 3 / 256) * 256)
EPS = 1e-6
N_IN_PIECES = 7
PIECE_W = 512

FF_CHUNK = 256
N_FF_CHUNKS = D_FF // FF_CHUNK
SUBLANES = 8
DIAG_BLOCK = SUBLANES
LEVEL_BLOCKS = (8, 16, 32)

HGRN_PIECES_PER_SLOT = 4
TILE_A = 512
TILE_C = 512
VMEM_LIMIT_BYTES = 56 * 1024 * 1024


def _dot(a, b):
    return jnp.dot(a, b, preferred_element_type=F32)


def _dot_nt(a, b):
    return lax.dot_general(a, b, (((1,), (1,)), ((), ())), preferred_element_type=F32)


def _dot_tn(a, b):
    return lax.dot_general(a, b, (((0,), (0,)), ((), ())), preferred_element_type=F32)


def _rms(x, g):
    ms = jnp.mean(x * x, axis=-1, keepdims=True)
    return x * lax.rsqrt(ms + EPS) * g


def _silu(x):
    return x * jax.nn.sigmoid(x)


def _split3(x):
    hi = x.astype(BF16)
    r1 = x - hi.astype(F32)
    mid = r1.astype(BF16)
    r2 = r1 - mid.astype(F32)
    lo = r2.astype(BF16)
    return hi, mid, lo


K_TILE = 256


def _swiglu_into(h_ref, wg_ref, wu_ref, wd_ref, acc_ref, side_work=None, pieces_per_slot=0):
    def pull():
        for _ in range(pieces_per_slot):
            next(side_work, None)

    def dot_ktiles(w_ref, cols):
        out = None
        for kt in range(D_MODEL // K_TILE):
            ks = slice(kt * K_TILE, (kt + 1) * K_TILE)
            part = _dot(h_ref[:, ks], w_ref[ks, cols])
            out = part if out is None else out + part
            pull()
        return out

    for c in range(N_FF_CHUNKS):
        cols = slice(c * FF_CHUNK, (c + 1) * FF_CHUNK)
        g = dot_ktiles(wg_ref, cols)
        u = dot_ktiles(wu_ref, cols)
        a = (_silu(g) * u).astype(BF16)
        for nt in range(D_MODEL // K_TILE):
            ns = slice(nt * K_TILE, (nt + 1) * K_TILE)
            d = _dot(a, wd_ref[cols, ns])
            if c == 0:
                acc_ref[:, ns] = d
            else:
                acc_ref[:, ns] += d
            pull()
    if side_work is not None:
        for _ in side_work:
            pass


def _intra_chunk_scores(q, k, b, masks):
    diag_mask, level_masks, odd_rows, lane_blk = masks
    level_scores = []
    for bs, odd in zip(LEVEL_BLOCKS, odd_rows):
        npair = CHUNK // (2 * bs)
        r = b.reshape(npair, 2 * bs, HGRN_DK)[:, bs:bs + 1, :]
        r = jnp.broadcast_to(r, (npair, 2 * bs, HGRN_DK)).reshape(CHUNK, HGRN_DK)
        d = b - r
        e = jnp.exp(jnp.where(odd, d, -d))
        ql = jnp.where(odd, q * e, 0.0).astype(BF16)
        kl = jnp.where(odd, 0.0, k * e).astype(BF16)
        level_scores.append(_dot_nt(ql, kl))
        yield
    nblk = CHUNK // DIAG_BLOCK
    q3 = q.reshape(nblk, DIAG_BLOCK, HGRN_DK)
    k3 = k.reshape(nblk, DIAG_BLOCK, HGRN_DK)
    b3 = b.reshape(nblk, DIAG_BLOCK, HGRN_DK)
    s3 = jnp.zeros((nblk, DIAG_BLOCK, CHUNK), F32)
    for s in range(DIAG_BLOCK):
        kb = k3[:, s:s + 1, :]
        bb = b3[:, s:s + 1, :]
        tmp = q3 * kb * jnp.exp(b3 - bb)
        red = jnp.sum(tmp, axis=-1, keepdims=True)
        s3 = jnp.where(lane_blk == s, red, s3)
        yield
    scores = jnp.where(diag_mask, s3.reshape(CHUNK, CHUNK), 0.0)
    for lmask, ls in zip(level_masks, level_scores):
        scores = jnp.where(lmask, ls, scores)
    return scores


def _chunk_masks():
    row = lax.broadcasted_iota(jnp.int32, (CHUNK, CHUNK), 0)
    col = lax.broadcasted_iota(jnp.int32, (CHUNK, CHUNK), 1)
    diag_mask = (row // DIAG_BLOCK == col // DIAG_BLOCK) & (col <= row)
    rk = lax.broadcasted_iota(jnp.int32, (CHUNK, HGRN_DK), 0)
    level_masks, odd_rows = [], []
    for bs in LEVEL_BLOCKS:
        level_masks.append((row // (2 * bs) == col // (2 * bs))
                           & ((row // bs) % 2 == 1) & ((col // bs) % 2 == 0))
        odd_rows.append((rk // bs) % 2 == 1)
    nblk = CHUNK // DIAG_BLOCK
    lane = lax.broadcasted_iota(jnp.int32, (nblk, DIAG_BLOCK, CHUNK), 2)
    blk = lax.broadcasted_iota(jnp.int32, (nblk, DIAG_BLOCK, CHUNK), 0)
    lane_blk = lane - blk * DIAG_BLOCK
    return diag_mask, level_masks, odd_rows, lane_blk


def _hgrn_unit(slot, c, h, masks, q_s, k_s, b_s, v_s, g_s, gh_ref, y_ref, st_ref):
    rows = slice(c * CHUNK, (c + 1) * CHUNK)
    lanes = slice(h * HGRN_DK, (h + 1) * HGRN_DK)
    vl = slice(h * HGRN_DV, (h + 1) * HGRN_DV)
    q = q_s[slot, rows, lanes]
    k = k_s[slot, rows, lanes]
    b = b_s[slot, rows, lanes]
    vb = v_s[slot, rows, vl].astype(BF16)
    b_last = b[CHUNK - 1:CHUNK, :]
    st = st_ref[h]
    o_inter = _dot_nt((q * jnp.exp(b)).astype(BF16), st.astype(BF16))
    kd = (k * jnp.exp(b_last - b)).astype(BF16)
    upd = _dot_tn(vb, kd)
    yield
    st_ref[h] = st * jnp.exp(b_last) + upd
    yield
    scores = yield from _intra_chunk_scores(q, k, b, masks)
    o_intra = _dot(scores.astype(BF16), vb)
    yield
    o = o_inter + o_intra
    o = o * lax.rsqrt(jnp.mean(o * o, axis=-1, keepdims=True) + EPS) * gh_ref[:, vl]
    y_ref[rows, vl] = o * g_s[slot, rows, vl]
    yield


def _hgrn_pieces(slot, q_s, k_s, b_s, v_s, g_s, gh_ref, y_ref, st_ref):
    masks = _chunk_masks()
    for c in range(q_s.shape[1] // CHUNK):
        units = [_hgrn_unit(slot, c, h, masks, q_s, k_s, b_s, v_s, g_s, gh_ref, y_ref, st_ref)
                 for h in range(HGRN_HEADS)]
        while units:
            for u in list(units):
                try:
                    next(u)
                    yield
                except StopIteration:
                    units.remove(u)


def _inproj_tile(slot, side_work, x_ref, g1_ref, wg_ref, wu_ref, wd_ref, gm_ref, win_ref, lbp_ref,
                 cw_ref, tri_ref, x1_ref, yc_ref, h_ref, acc_ref, ubuf_ref, q_s, k_s, b_s, v_s, g_s):
    tm = x_ref.shape[0]
    x = x_ref[...]
    h_ref[...] = _rms(x, g1_ref[...]).astype(BF16)
    _swiglu_into(h_ref, wg_ref, wu_ref, wd_ref, acc_ref, side_work, HGRN_PIECES_PER_SLOT)
    x1 = x + 0.5 * acc_ref[...]
    x1_ref[...] = x1
    h_ref[...] = _rms(x1, gm_ref[...]).astype(BF16)

    def proj(p):
        return _dot(h_ref[...], win_ref[:, p * PIECE_W:(p + 1) * PIECE_W])

    lbp = lbp_ref[...]
    lbe = jnp.exp(lbp - jnp.max(lbp, axis=0, keepdims=True))
    lb = lbe[0:1, :] / jnp.sum(lbe, axis=0, keepdims=True)

    q_s[slot] = _silu(proj(0)) * (HGRN_DK ** -0.5)

    f = lb + (1.0 - lb) * jax.nn.sigmoid(proj(1))
    k_s[slot] = 1.0 - f
    logf = jnp.log(f)
    hi, mid, lo = _split3(logf)
    tri = tri_ref[...]
    b_s[slot] = _dot(tri, hi) + _dot(tri, mid) + _dot(tri, lo)

    v_s[slot] = proj(2)
    g_s[slot] = _silu(proj(3))

    u = proj(5) * proj(6)
    ubuf_ref[SUBLANES:SUBLANES + tm, :] = u
    u1 = ubuf_ref[pl.ds(SUBLANES - 1, tm), :]
    u2 = ubuf_ref[pl.ds(SUBLANES - 2, tm), :]
    cw = cw_ref[...]
    yc_ref[...] = proj(4) * (cw[0:1, :] * u2 + cw[1:2, :] * u1 + cw[2:3, :] * u)
    ubuf_ref[0:SUBLANES, :] = ubuf_ref[tm:tm + SUBLANES, :]


def _stage_ab_kernel(x_ref, g1_ref, wg_ref, wu_ref, wd_ref, gm_ref, win_ref, lbp_ref, cw_ref,
                     tri_ref, gh_ref,
                     x1_ref, yh_ref, yc_ref,
                     h_ref, acc_ref, ubuf_ref, q_s, k_s, b_s, v_s, g_s, st_ref,
                     *, tiles_per_seq, n_tiles):
    i = pl.program_id(0)
    wslot = i % 2
    rslot = 1 - wslot
    staging = (q_s, k_s, b_s, v_s, g_s)

    @pl.when(i == 0)
    def _():
        for ref in staging:
            ref[rslot] = jnp.zeros(ref.shape[1:], F32)

    @pl.when((i + tiles_per_seq - 1) % tiles_per_seq == 0)
    def _():
        st_ref[...] = jnp.zeros_like(st_ref)

    @pl.when(i % tiles_per_seq == 0)
    def _():
        ubuf_ref[0:SUBLANES, :] = jnp.zeros((SUBLANES, CONV_W), F32)

    @pl.when(i < n_tiles)
    def _():
        hgrn = _hgrn_pieces(rslot, *staging, gh_ref, yh_ref, st_ref)
        _inproj_tile(wslot, hgrn, x_ref, g1_ref, wg_ref, wu_ref, wd_ref, gm_ref, win_ref, lbp_ref,
                     cw_ref, tri_ref, x1_ref, yc_ref, h_ref, acc_ref, ubuf_ref, *staging)

    @pl.when(i == n_tiles)
    def _():
        for _ in _hgrn_pieces(rslot, *staging, gh_ref, yh_ref, st_ref):
            pass


def _const_spec(shape):
    nd = len(shape)
    return pl.BlockSpec(shape, lambda *_: (0,) * nd, pipeline_mode=pl.Buffered(1))


def _stage_ab(x2d, g1, wg, wu, wd, gm, win, lbp, cw, tri, gh, *, seq):
    t = x2d.shape[0]
    tm = TILE_A
    n_tiles = t // tm
    cur = lambda w: pl.BlockSpec((tm, w), lambda i: (jnp.minimum(i, n_tiles - 1), 0))
    prev = lambda w: pl.BlockSpec((tm, w), lambda i: (jnp.maximum(i - 1, 0), 0))
    consts = (g1, wg, wu, wd, gm, win, lbp, cw, tri, gh)
    stage = pltpu.VMEM((2, tm, PIECE_W), F32)
    return pl.pallas_call(
        functools.partial(_stage_ab_kernel, tiles_per_seq=seq // tm, n_tiles=n_tiles),
        grid=(n_tiles + 1,),
        in_specs=[cur(D_MODEL)] + [_const_spec(c.shape) for c in consts],
        out_specs=[cur(D_MODEL), prev(HGRN_W), cur(CONV_W)],
        out_shape=[jax.ShapeDtypeStruct((t, D_MODEL), F32), jax.ShapeDtypeStruct((t, HGRN_W), F32),
                   jax.ShapeDtypeStruct((t, CONV_W), F32)],
        scratch_shapes=[pltpu.VMEM((tm, D_MODEL), BF16), pltpu.VMEM((tm, D_MODEL), F32),
                        pltpu.VMEM((tm + SUBLANES, CONV_W), F32),
                        stage, stage, stage, stage, stage,
                        pltpu.VMEM((HGRN_HEADS, HGRN_DV, HGRN_DK), F32)],
        compiler_params=pltpu.CompilerParams(dimension_semantics=("arbitrary",),
                                             vmem_limit_bytes=VMEM_LIMIT_BYTES),
        name="stage_ab_ffn1_inproj_hgrn2",
    )(x2d, *consts)


def _stage_kv_kernel(mem_ref, gn_ref, wkv_ref, k_ref, v_ref):
    mn = _rms(mem_ref[...], gn_ref[...]).astype(BF16)
    kv = _dot(mn, wkv_ref[...])
    k_ref[...] = kv[:, :D_MODEL].astype(BF16)
    v_ref[...] = kv[:, D_MODEL:].astype(BF16)


def _stage_kv(mem2d, gn, wkv, *, batch):
    blk = pl.BlockSpec((MEM_LEN, D_MODEL), lambda bi: (bi, 0))
    shp = jax.ShapeDtypeStruct((batch * MEM_LEN, D_MODEL), BF16)
    return pl.pallas_call(
        _stage_kv_kernel,
        grid=(batch,),
        in_specs=[blk, _const_spec(gn.shape), _const_spec(wkv.shape)],
        out_specs=[blk, blk],
        out_shape=[shp, shp],
        compiler_params=pltpu.CompilerParams(dimension_semantics=("arbitrary",),
                                             vmem_limit_bytes=VMEM_LIMIT_BYTES),
        name="stage_kv_mem",
    )(mem2d, gn, wkv)


def _stage_c_kernel(x1_ref, yh_ref, yc_ref, km_ref, vm_ref, wout_ref, gx_ref, wq_ref, wo_ref,
                    g2_ref, wg_ref, wu_ref, wd_ref, gf_ref,
                    out_ref,
                    h_ref, acc_ref, att_ref):
    x2 = (x1_ref[...]
          + _dot(yh_ref[...].astype(BF16), wout_ref[0:HGRN_W, :])
          + _dot(yc_ref[...].astype(BF16), wout_ref[HGRN_W:HGRN_W + CONV_W, :]))
    hq = _rms(x2, gx_ref[...]).astype(BF16)
    qm = _dot(hq, wq_ref[...]) * (MEM_HD ** -0.5)
    for h in range(MEM_HEADS):
        hs = slice(h * MEM_HD, (h + 1) * MEM_HD)
        sc = _dot_nt(qm[:, hs].astype(BF16), km_ref[:, hs])
        e = jnp.exp(sc - jnp.max(sc, axis=-1, keepdims=True))
        pv = _dot(e.astype(BF16), vm_ref[:, hs])
        att_ref[:, hs] = (pv / jnp.sum(e, axis=-1, keepdims=True)).astype(BF16)
    x3 = x2 + _dot(att_ref[...], wo_ref[...])
    h_ref[...] = _rms(x3, g2_ref[...]).astype(BF16)
    _swiglu_into(h_ref, wg_ref, wu_ref, wd_ref, acc_ref)
    x4 = x3 + 0.5 * acc_ref[...]
    out_ref[...] = _rms(x4, gf_ref[...])


def _stage_c(x1, yh, yc, km, vm, wout, gx, wq, wo, g2, wg, wu, wd, gf, *, seq):
    t = x1.shape[0]
    tm = TILE_C
    tiles_per_seq = seq // tm
    row = lambda w: pl.BlockSpec((tm, w), lambda i: (i, 0))
    memblk = pl.BlockSpec((MEM_LEN, D_MODEL), lambda i: (i // tiles_per_seq, 0))
    return pl.pallas_call(
        _stage_c_kernel,
        grid=(t // tm,),
        in_specs=[row(D_MODEL), row(HGRN_W), row(CONV_W), memblk, memblk,
                  _const_spec(wout.shape), _const_spec(gx.shape), _const_spec(wq.shape),
                  _const_spec(wo.shape), _const_spec(g2.shape), _const_spec(wg.shape),
                  _const_spec(wu.shape), _const_spec(wd.shape), _const_spec(gf.shape)],
        out_specs=row(D_MODEL),
        out_shape=jax.ShapeDtypeStruct((t, D_MODEL), F32),
        scratch_shapes=[pltpu.VMEM((tm, D_MODEL), BF16), pltpu.VMEM((tm, D_MODEL), F32),
                        pltpu.VMEM((tm, D_MODEL), BF16)],
        compiler_params=pltpu.CompilerParams(dimension_semantics=("arbitrary",),
                                             vmem_limit_bytes=VMEM_LIMIT_BYTES),
        name="stage_c_outproj_xattn_ffn2",
    )(x1, yh, yc, km, vm, wout, gx, wq, wo, g2, wg, wu, wd, gf)


def _ffn_weights(w_gate, w_up, w_down):
    return w_gate.astype(BF16), w_up.astype(BF16), w_down.astype(BF16)


def _block_tri(n):
    r = lax.broadcasted_iota(jnp.int32, (n, n), 0)
    c = lax.broadcasted_iota(jnp.int32, (n, n), 1)
    return ((r // CHUNK == c // CHUNK) & (c <= r)).astype(BF16)


def kernel(x, mem, ffn1_norm, ffn1_gate, ffn1_up, ffn1_down, mix_norm, w_in, lb_param, hgrn_out_norm, conv_w, w_out, xattn_norm, mem_norm, w_q_mem, w_kv_mem, w_o_mem, ffn2_norm, ffn2_gate, ffn2_up, ffn2_down, final_norm):
    batch, seq, _ = x.shape
    depth = ffn1_norm.shape[0]
    assert depth == 1 and seq % TILE_A == 0 and seq % TILE_C == 0
    l = 0
    x2d = x.reshape(batch * seq, D_MODEL)
    vec = lambda a: a.reshape(1, -1)

    wg1, wu1, wd1 = _ffn_weights(ffn1_gate[l], ffn1_up[l], ffn1_down[l])
    x1, yh, yc = _stage_ab(
        x2d, vec(ffn1_norm[l]), wg1, wu1, wd1, vec(mix_norm[l]), w_in[l].astype(BF16),
        lb_param, conv_w[l].T, _block_tri(TILE_A), vec(hgrn_out_norm[l]), seq=seq)

    km, vm = _stage_kv(mem.reshape(batch * MEM_LEN, D_MODEL), vec(mem_norm[l]),
                       w_kv_mem[l].astype(BF16), batch=batch)

    wg2, wu2, wd2 = _ffn_weights(ffn2_gate[l], ffn2_up[l], ffn2_down[l])
    out = _stage_c(x1, yh, yc, km, vm, w_out[l].astype(BF16), vec(xattn_norm[l]),
                   w_q_mem[l].astype(BF16), w_o_mem[l].astype(BF16), vec(ffn2_norm[l]),
                   wg2, wu2, wd2, vec(final_norm), seq=seq)
    return out.reshape(batch, seq, D_MODEL)
```

```python
import functools
import math

import jax
import jax.numpy as jnp
from jax import lax
from jax.experimental import pallas as pl
from jax.experimental.pallas import tpu as pltpu

F32 = jnp.float32
BF16 = jnp.bfloat16

D_MODEL = 1024
HGRN_W = 512
CONV_W = 512
HGRN_HEADS = 4
HGRN_DK = 128
HGRN_DV = 128
HGRN_F = HGRN_HEADS * HGRN_DK
CONV_K = 3
CHUNK = 64
MEM_LEN = 256
MEM_HEADS = 4
MEM_HD = D_MODEL // MEM_HEADS
D_FF = int(math.ceil(8 * D_MODEL / 3 / 256) * 256)
EPS = 1e-6
LOG2_E = 1.4426950408889634
N_IN_PIECES = 7
PIECE_W = 512

FF_CHUNK = 256
N_FF_CHUNKS = D_FF // FF_CHUNK
SUBLANES = 8
DIAG_BLOCK = SUBLANES
LEVEL_BLOCKS = (8, 16, 32)

TILE_A = 512
TILE_C = 512
VMEM_LIMIT_BYTES = 56 * 1024 * 1024


def _dot(a, b):
    return jnp.dot(a, b, preferred_element_type=F32)


def _dot_nt(a, b):
    return lax.dot_general(a, b, (((1,), (1,)), ((), ())), preferred_element_type=F32)


def _dot_tn(a, b):
    return lax.dot_general(a, b, (((0,), (0,)), ((), ())), preferred_element_type=F32)


def _rms(x, g):
    ms = jnp.mean(x * x, axis=-1, keepdims=True)
    return x * lax.rsqrt(ms + EPS) * g


def _silu(x):
    return x * jax.nn.sigmoid(x)


def _split3(x):
    hi = x.astype(BF16)
    r1 = x - hi.astype(F32)
    mid = r1.astype(BF16)
    r2 = r1 - mid.astype(F32)
    lo = r2.astype(BF16)
    return hi, mid, lo


K_TILE = 256


def _swiglu_into(h_ref, wg_ref, wu_ref, wd_ref, acc_ref, side_work=None, n_side_steps=0):
    n_slots = N_FF_CHUNKS * 3 * (D_MODEL // K_TILE)
    slot_counter = [0, 0]

    def pull():
        slot_counter[0] += 1
        due = -(-n_side_steps * slot_counter[0] // n_slots)
        while slot_counter[1] < due:
            next(side_work, None)
            slot_counter[1] += 1

    def dot_ktiles(w_ref, cols):
        out = None
        for kt in range(D_MODEL // K_TILE):
            ks = slice(kt * K_TILE, (kt + 1) * K_TILE)
            part = _dot(h_ref[:, ks], w_ref[ks, cols])
            out = part if out is None else out + part
            pull()
        return out

    for c in range(N_FF_CHUNKS):
        cols = slice(c * FF_CHUNK, (c + 1) * FF_CHUNK)
        g = dot_ktiles(wg_ref, cols)
        u = dot_ktiles(wu_ref, cols)
        a = (_silu(g) * u).astype(BF16)
        for nt in range(D_MODEL // K_TILE):
            ns = slice(nt * K_TILE, (nt + 1) * K_TILE)
            d = _dot(a, wd_ref[cols, ns])
            if c == 0:
                acc_ref[:, ns] = d
            else:
                acc_ref[:, ns] += d
            pull()


def _chunk_masks():
    row = lax.broadcasted_iota(jnp.int32, (CHUNK, CHUNK), 0)
    col = lax.broadcasted_iota(jnp.int32, (CHUNK, CHUNK), 1)
    diag_mask = (row // DIAG_BLOCK == col // DIAG_BLOCK) & (col <= row)
    rk = lax.broadcasted_iota(jnp.int32, (CHUNK, HGRN_DK), 0)
    level_masks, odd_rows = [], []
    for bs in LEVEL_BLOCKS:
        level_masks.append((row // (2 * bs) == col // (2 * bs))
                           & ((row // bs) % 2 == 1) & ((col // bs) % 2 == 0))
        odd_rows.append((rk // bs) % 2 == 1)
    nblk = CHUNK // DIAG_BLOCK
    lane = lax.broadcasted_iota(jnp.int32, (nblk, DIAG_BLOCK, CHUNK), 2)
    blk = lax.broadcasted_iota(jnp.int32, (nblk, DIAG_BLOCK, CHUNK), 0)
    lane_blk = lane - blk * DIAG_BLOCK
    return diag_mask, level_masks, odd_rows, lane_blk


HGRN_UNIT_STEPS = 6 + len(LEVEL_BLOCKS) + DIAG_BLOCK


def _hgrn_unit(slot, c, h, masks, q_s, k_s, b_s, v_s, g_s, gh_ref, y_ref, st_ref):
    diag_mask, level_masks, odd_rows, lane_blk = masks
    rows = slice(c * CHUNK, (c + 1) * CHUNK)
    lanes = slice(h * HGRN_DK, (h + 1) * HGRN_DK)
    vl = slice(h * HGRN_DV, (h + 1) * HGRN_DV)
    q = q_s[slot, rows, lanes]
    kbb = k_s[slot, rows, lanes]
    b2 = b_s[slot, rows, lanes]
    v = v_s[slot, rows, vl]
    vb = v.astype(BF16)
    vbt = v.T.astype(BF16)
    b_last = b2[CHUNK - 1:CHUNK, :]
    qb = (q * jnp.exp2(b2)).astype(BF16)
    kd = jnp.exp2(b_last - kbb).astype(BF16)
    yield
    st = st_ref[h]
    o_inter = _dot_nt(qb, st.astype(BF16))
    upd = _dot(vbt, kd)
    yield

    def level_operands(bs, odd):
        npair = CHUNK // (2 * bs)
        r = b2.reshape(npair, 2 * bs, HGRN_DK)[:, bs:bs + 1, :]
        r = jnp.broadcast_to(r, (npair, 2 * bs, HGRN_DK)).reshape(CHUNK, HGRN_DK)
        e = jnp.exp2(jnp.where(odd, b2 - r, r - kbb))
        return (q * e).astype(BF16), e.astype(BF16)

    operands = level_operands(LEVEL_BLOCKS[0], odd_rows[0])
    st_ref[h] = st * jnp.exp2(b_last) + upd
    yield
    level_scores = []
    for nxt in range(1, len(LEVEL_BLOCKS) + 1):
        level_scores.append(_dot_nt(*operands))
        if nxt < len(LEVEL_BLOCKS):
            operands = level_operands(LEVEL_BLOCKS[nxt], odd_rows[nxt])
        yield
    nblk = CHUNK // DIAG_BLOCK
    q3 = q.reshape(nblk, DIAG_BLOCK, HGRN_DK)
    b3 = b2.reshape(nblk, DIAG_BLOCK, HGRN_DK)
    kbb3 = kbb.reshape(nblk, DIAG_BLOCK, HGRN_DK)
    s3 = jnp.zeros((nblk, DIAG_BLOCK, CHUNK), F32)
    for s in range(DIAG_BLOCK):
        tmp = q3 * jnp.exp2(b3 - kbb3[:, s:s + 1, :])
        red = jnp.sum(tmp, axis=-1, keepdims=True)
        s3 = jnp.where(lane_blk == s, red, s3)
        yield
    scores = jnp.where(diag_mask, s3.reshape(CHUNK, CHUNK), 0.0)
    for lmask, ls in zip(level_masks, level_scores):
        scores = jnp.where(lmask, ls, scores)
    scores = scores.astype(BF16)
    yield
    o_intra = _dot(scores, vb)
    yield
    o = o_inter + o_intra
    o = o * lax.rsqrt(jnp.mean(o * o, axis=-1, keepdims=True) + EPS) * gh_ref[:, vl]
    y_ref[rows, vl] = o * g_s[slot, rows, vl]
    yield


def _hgrn_pieces(slot, q_s, k_s, b_s, v_s, g_s, gh_ref, y_ref, st_ref):
    masks = _chunk_masks()
    for c in range(q_s.shape[1] // CHUNK):
        units = [_hgrn_unit(slot, c, h, masks, q_s, k_s, b_s, v_s, g_s, gh_ref, y_ref, st_ref)
                 for h in range(HGRN_HEADS)]
        while units:
            for u in list(units):
                try:
                    next(u)
                    yield
                except StopIteration:
                    units.remove(u)


def _inproj_tile(slot, side_work, n_side_steps, x_ref, g1_ref, wg_ref, wu_ref, wd_ref, gm_ref, win_ref, lbp_ref,
                 cw_ref, tri_ref, x1_ref, yc_ref, h_ref, acc_ref, ubuf_ref, q_s, k_s, b_s, v_s, g_s):
    tm = x_ref.shape[0]
    x = x_ref[...]
    h_ref[...] = _rms(x, g1_ref[...]).astype(BF16)
    _swiglu_into(h_ref, wg_ref, wu_ref, wd_ref, acc_ref, side_work, n_side_steps)
    x1 = x + 0.5 * acc_ref[...]
    x1_ref[...] = x1
    h_ref[...] = _rms(x1, gm_ref[...]).astype(BF16)

    def proj(p):
        return _dot(h_ref[...], win_ref[:, p * PIECE_W:(p + 1) * PIECE_W])

    lbp = lbp_ref[...]
    lbe = jnp.exp(lbp - jnp.max(lbp, axis=0, keepdims=True))
    lb = lbe[0:1, :] / jnp.sum(lbe, axis=0, keepdims=True)

    q_s[slot] = _silu(proj(0)) * (HGRN_DK ** -0.5)

    f = lb + (1.0 - lb) * jax.nn.sigmoid(proj(1))
    log2f = jnp.log(f) * LOG2_E
    hi, mid, lo = _split3(log2f)
    tri = tri_ref[...]
    b2 = _dot(tri, hi) + _dot(tri, mid) + _dot(tri, lo)
    b_s[slot] = b2
    k_s[slot] = b2 - jnp.log(1.0 - f) * LOG2_E

    v_s[slot] = proj(2)
    g_s[slot] = _silu(proj(3))

    u = proj(5) * proj(6)
    ubuf_ref[SUBLANES:SUBLANES + tm, :] = u
    u1 = ubuf_ref[pl.ds(SUBLANES - 1, tm), :]
    u2 = ubuf_ref[pl.ds(SUBLANES - 2, tm), :]
    cw = cw_ref[...]
    yc_ref[...] = proj(4) * (cw[0:1, :] * u2 + cw[1:2, :] * u1 + cw[2:3, :] * u)
    ubuf_ref[0:SUBLANES, :] = ubuf_ref[tm:tm + SUBLANES, :]
    for _ in side_work:
        pass


def _stage_ab_kernel(x_ref, g1_ref, wg_ref, wu_ref, wd_ref, gm_ref, win_ref, lbp_ref, cw_ref,
                     tri_ref, gh_ref,
                     x1_ref, yh_ref, yc_ref,
                     h_ref, acc_ref, ubuf_ref, q_s, k_s, b_s, v_s, g_s, st_ref,
                     *, tiles_per_seq, n_tiles):
    i = pl.program_id(0)
    wslot = i % 2
    rslot = 1 - wslot
    staging = (q_s, k_s, b_s, v_s, g_s)

    @pl.when(i == 0)
    def _():
        for ref in staging:
            ref[rslot] = jnp.zeros(ref.shape[1:], F32)

    @pl.when((i + tiles_per_seq - 1) % tiles_per_seq == 0)
    def _():
        st_ref[...] = jnp.zeros_like(st_ref)

    @pl.when(i % tiles_per_seq == 0)
    def _():
        ubuf_ref[0:SUBLANES, :] = jnp.zeros((SUBLANES, CONV_W), F32)

    @pl.when(i < n_tiles)
    def _():
        hgrn = _hgrn_pieces(rslot, *staging, gh_ref, yh_ref, st_ref)
        _inproj_tile(wslot, hgrn, (TILE_A // CHUNK) * HGRN_HEADS * HGRN_UNIT_STEPS, x_ref, g1_ref, wg_ref, wu_ref, wd_ref, gm_ref, win_ref, lbp_ref,
                     cw_ref, tri_ref, x1_ref, yc_ref, h_ref, acc_ref, ubuf_ref, *staging)

    @pl.when(i == n_tiles)
    def _():
        for _ in _hgrn_pieces(rslot, *staging, gh_ref, yh_ref, st_ref):
            pass


def _const_spec(shape):
    nd = len(shape)
    return pl.BlockSpec(shape, lambda *_: (0,) * nd, pipeline_mode=pl.Buffered(1))


def _stage_ab(x2d, g1, wg, wu, wd, gm, win, lbp, cw, tri, gh, *, seq):
    t = x2d.shape[0]
    tm = TILE_A
    n_tiles = t // tm
    cur = lambda w: pl.BlockSpec((tm, w), lambda i: (jnp.minimum(i, n_tiles - 1), 0))
    prev = lambda w: pl.BlockSpec((tm, w), lambda i: (jnp.maximum(i - 1, 0), 0))
    consts = (g1, wg, wu, wd, gm, win, lbp, cw, tri, gh)
    stage = pltpu.VMEM((2, tm, PIECE_W), F32)
    return pl.pallas_call(
        functools.partial(_stage_ab_kernel, tiles_per_seq=seq // tm, n_tiles=n_tiles),
        grid=(n_tiles + 1,),
        in_specs=[cur(D_MODEL)] + [_const_spec(c.shape) for c in consts],
        out_specs=[cur(D_MODEL), prev(HGRN_W), cur(CONV_W)],
        out_shape=[jax.ShapeDtypeStruct((t, D_MODEL), F32), jax.ShapeDtypeStruct((t, HGRN_W), F32),
                   jax.ShapeDtypeStruct((t, CONV_W), F32)],
        scratch_shapes=[pltpu.VMEM((tm, D_MODEL), BF16), pltpu.VMEM((tm, D_MODEL), F32),
                        pltpu.VMEM((tm + SUBLANES, CONV_W), F32),
                        stage, stage, stage, stage, stage,
                        pltpu.VMEM((HGRN_HEADS, HGRN_DV, HGRN_DK), F32)],
        compiler_params=pltpu.CompilerParams(dimension_semantics=("arbitrary",),
                                             vmem_limit_bytes=VMEM_LIMIT_BYTES),
        name="stage_ab_ffn1_inproj_hgrn2",
    )(x2d, *consts)


def _stage_kv_kernel(mem_ref, gn_ref, wkv_ref, k_ref, v_ref):
    mn = _rms(mem_ref[...], gn_ref[...]).astype(BF16)
    kv = _dot(mn, wkv_ref[...])
    k_ref[...] = kv[:, :D_MODEL].astype(BF16)
    v_ref[...] = kv[:, D_MODEL:].astype(BF16)


def _stage_kv(mem2d, gn, wkv, *, batch):
    blk = pl.BlockSpec((MEM_LEN, D_MODEL), lambda bi: (bi, 0))
    shp = jax.ShapeDtypeStruct((batch * MEM_LEN, D_MODEL), BF16)
    return pl.pallas_call(
        _stage_kv_kernel,
        grid=(batch,),
        in_specs=[blk, _const_spec(gn.shape), _const_spec(wkv.shape)],
        out_specs=[blk, blk],
        out_shape=[shp, shp],
        compiler_params=pltpu.CompilerParams(dimension_semantics=("arbitrary",),
                                             vmem_limit_bytes=VMEM_LIMIT_BYTES),
        name="stage_kv_mem",
    )(mem2d, gn, wkv)


def _stage_c_kernel(x1_ref, yh_ref, yc_ref, km_ref, vm_ref, wout_ref, gx_ref, wq_ref, wo_ref,
                    g2_ref, wg_ref, wu_ref, wd_ref, gf_ref,
                    out_ref,
                    h_ref, acc_ref, att_ref):
    x2 = (x1_ref[...]
          + _dot(yh_ref[...].astype(BF16), wout_ref[0:HGRN_W, :])
          + _dot(yc_ref[...].astype(BF16), wout_ref[HGRN_W:HGRN_W + CONV_W, :]))
    hq = _rms(x2, gx_ref[...]).astype(BF16)
    qm = _dot(hq, wq_ref[...]) * (MEM_HD ** -0.5)
    for h in range(MEM_HEADS):
        hs = slice(h * MEM_HD, (h + 1) * MEM_HD)
        sc = _dot_nt(qm[:, hs].astype(BF16), km_ref[:, hs])
        e = jnp.exp(sc - jnp.max(sc, axis=-1, keepdims=True))
        pv = _dot(e.astype(BF16), vm_ref[:, hs])
        att_ref[:, hs] = (pv / jnp.sum(e, axis=-1, keepdims=True)).astype(BF16)
    x3 = x2 + _dot(att_ref[...], wo_ref[...])
    h_ref[...] = _rms(x3, g2_ref[...]).astype(BF16)
    _swiglu_into(h_ref, wg_ref, wu_ref, wd_ref, acc_ref)
    x4 = x3 + 0.5 * acc_ref[...]
    out_ref[...] = _rms(x4, gf_ref[...])


def _stage_c(x1, yh, yc, km, vm, wout, gx, wq, wo, g2, wg, wu, wd, gf, *, seq):
    t = x1.shape[0]
    tm = TILE_C
    tiles_per_seq = seq // tm
    row = lambda w: pl.BlockSpec((tm, w), lambda i: (i, 0))
    memblk = pl.BlockSpec((MEM_LEN, D_MODEL), lambda i: (i // tiles_per_seq, 0))
    return pl.pallas_call(
        _stage_c_kernel,
        grid=(t // tm,),
        in_specs=[row(D_MODEL), row(HGRN_W), row(CONV_W), memblk, memblk,
                  _const_spec(wout.shape), _const_spec(gx.shape), _const_spec(wq.shape),
                  _const_spec(wo.shape), _const_spec(g2.shape), _const_spec(wg.shape),
                  _const_spec(wu.shape), _const_spec(wd.shape), _const_spec(gf.shape)],
        out_specs=row(D_MODEL),
        out_shape=jax.ShapeDtypeStruct((t, D_MODEL), F32),
        scratch_shapes=[pltpu.VMEM((tm, D_MODEL), BF16), pltpu.VMEM((tm, D_MODEL), F32),
                        pltpu.VMEM((tm, D_MODEL), BF16)],
        compiler_params=pltpu.CompilerParams(dimension_semantics=("arbitrary",),
                                             vmem_limit_bytes=VMEM_LIMIT_BYTES),
        name="stage_c_outproj_xattn_ffn2",
    )(x1, yh, yc, km, vm, wout, gx, wq, wo, g2, wg, wu, wd, gf)


def _ffn_weights(w_gate, w_up, w_down):
    return w_gate.astype(BF16), w_up.astype(BF16), w_down.astype(BF16)


def _block_tri(n):
    r = lax.broadcasted_iota(jnp.int32, (n, n), 0)
    c = lax.broadcasted_iota(jnp.int32, (n, n), 1)
    return ((r // CHUNK == c // CHUNK) & (c <= r)).astype(BF16)


def kernel(x, mem, ffn1_norm, ffn1_gate, ffn1_up, ffn1_down, mix_norm, w_in, lb_param, hgrn_out_norm, conv_w, w_out, xattn_norm, mem_norm, w_q_mem, w_kv_mem, w_o_mem, ffn2_norm, ffn2_gate, ffn2_up, ffn2_down, final_norm):
    batch, seq, _ = x.shape
    depth = ffn1_norm.shape[0]
    assert depth == 1 and seq % TILE_A == 0 and seq % TILE_C == 0
    l = 0
    x2d = x.reshape(batch * seq, D_MODEL)
    vec = lambda a: a.reshape(1, -1)

    wg1, wu1, wd1 = _ffn_weights(ffn1_gate[l], ffn1_up[l], ffn1_down[l])
    x1, yh, yc = _stage_ab(
        x2d, vec(ffn1_norm[l]), wg1, wu1, wd1, vec(mix_norm[l]), w_in[l].astype(BF16),
        lb_param, conv_w[l].T, _block_tri(TILE_A), vec(hgrn_out_norm[l]), seq=seq)

    km, vm = _stage_kv(mem.reshape(batch * MEM_LEN, D_MODEL), vec(mem_norm[l]),
                       w_kv_mem[l].astype(BF16), batch=batch)

    wg2, wu2, wd2 = _ffn_weights(ffn2_gate[l], ffn2_up[l], ffn2_down[l])
    out = _stage_c(x1, yh, yc, km, vm, w_out[l].astype(BF16), vec(xattn_norm[l]),
                   w_q_mem[l].astype(BF16), w_o_mem[l].astype(BF16), vec(ffn2_norm[l]),
                   wg2, wu2, wd2, vec(final_norm), seq=seq)
    return out.reshape(batch, seq, D_MODEL)
```

```python
import functools
import math

import jax
import jax.numpy as jnp
from jax import lax
from jax.experimental import pallas as pl
from jax.experimental.pallas import tpu as pltpu

F32 = jnp.float32
BF16 = jnp.bfloat16

D_MODEL = 1024
HGRN_W = 512
CONV_W = 512
HGRN_HEADS = 4
HGRN_DK = 128
HGRN_DV = 128
HGRN_F = HGRN_HEADS * HGRN_DK
CONV_K = 3
CHUNK = 64
MEM_LEN = 256
MEM_HEADS = 4
MEM_HD = D_MODEL // MEM_HEADS
D_FF = int(math.ceil(8 * D_MODEL / 3 / 256) * 256)
EPS = 1e-6
LOG2_E = 1.4426950408889634
N_IN_PIECES = 7
PIECE_W = 512

FF_CHUNK = 256
N_FF_CHUNKS = D_FF // FF_CHUNK
SUBLANES = 8
DIAG_BLOCK = SUBLANES
LEVEL_BLOCKS = (8, 16, 32)

TILE_A = 512
TILE_C = 512
VMEM_LIMIT_BYTES = 56 * 1024 * 1024


def _dot(a, b):
    return jnp.dot(a, b, preferred_element_type=F32)


def _dot_nt(a, b):
    return lax.dot_general(a, b, (((1,), (1,)), ((), ())), preferred_element_type=F32)


def _dot_tn(a, b):
    return lax.dot_general(a, b, (((0,), (0,)), ((), ())), preferred_element_type=F32)


def _rms(x, g):
    ms = jnp.mean(x * x, axis=-1, keepdims=True)
    return x * lax.rsqrt(ms + EPS) * g


def _silu(x):
    return x * jax.nn.sigmoid(x)


def _split3(x):
    hi = x.astype(BF16)
    r1 = x - hi.astype(F32)
    mid = r1.astype(BF16)
    r2 = r1 - mid.astype(F32)
    lo = r2.astype(BF16)
    return hi, mid, lo


K_TILE = 256


def _swiglu_into(h_ref, wg_ref, wu_ref, wd_ref, acc_ref, side_work=None, n_side_steps=0):
    n_slots = N_FF_CHUNKS * 3 * (D_MODEL // K_TILE)
    slot_counter = [0, 0]

    def pull():
        slot_counter[0] += 1
        due = -(-n_side_steps * slot_counter[0] // n_slots)
        while slot_counter[1] < due:
            next(side_work, None)
            slot_counter[1] += 1

    def dot_ktiles(w_ref, cols):
        out = None
        for kt in range(D_MODEL // K_TILE):
            ks = slice(kt * K_TILE, (kt + 1) * K_TILE)
            part = _dot(h_ref[:, ks], w_ref[ks, cols])
            out = part if out is None else out + part
            pull()
        return out

    for c in range(N_FF_CHUNKS):
        cols = slice(c * FF_CHUNK, (c + 1) * FF_CHUNK)
        g = dot_ktiles(wg_ref, cols)
        u = dot_ktiles(wu_ref, cols)
        a = (_silu(g) * u).astype(BF16)
        for nt in range(D_MODEL // K_TILE):
            ns = slice(nt * K_TILE, (nt + 1) * K_TILE)
            d = _dot(a, wd_ref[cols, ns])
            if c == 0:
                acc_ref[:, ns] = d
            else:
                acc_ref[:, ns] += d
            pull()


def _chunk_masks():
    row = lax.broadcasted_iota(jnp.int32, (CHUNK, CHUNK), 0)
    col = lax.broadcasted_iota(jnp.int32, (CHUNK, CHUNK), 1)
    diag_mask = (row // DIAG_BLOCK == col // DIAG_BLOCK) & (col <= row)
    rk = lax.broadcasted_iota(jnp.int32, (CHUNK, HGRN_DK), 0)
    level_masks, odd_rows = [], []
    for bs in LEVEL_BLOCKS:
        level_masks.append((row // (2 * bs) == col // (2 * bs))
                           & ((row // bs) % 2 == 1) & ((col // bs) % 2 == 0))
        odd_rows.append((rk // bs) % 2 == 1)
    nblk = CHUNK // DIAG_BLOCK
    lane = lax.broadcasted_iota(jnp.int32, (nblk, DIAG_BLOCK, CHUNK), 2)
    blk = lax.broadcasted_iota(jnp.int32, (nblk, DIAG_BLOCK, CHUNK), 0)
    lane_blk = lane - blk * DIAG_BLOCK
    return diag_mask, level_masks, odd_rows, lane_blk


HGRN_UNIT_STEPS = 6 + len(LEVEL_BLOCKS) + DIAG_BLOCK


def _hgrn_unit(slot, c, h, masks, q_s, k_s, b_s, v_s, g_s, gh_ref, y_ref, st_ref):
    diag_mask, level_masks, odd_rows, lane_blk = masks
    rows = slice(c * CHUNK, (c + 1) * CHUNK)
    lanes = slice(h * HGRN_DK, (h + 1) * HGRN_DK)
    vl = slice(h * HGRN_DV, (h + 1) * HGRN_DV)
    q = q_s[slot, rows, lanes]
    kbb = k_s[slot, rows, lanes]
    b2 = b_s[slot, rows, lanes]
    v = v_s[slot, rows, vl]
    vb = v.astype(BF16)
    vbt = v.T.astype(BF16)
    b_last = b2[CHUNK - 1:CHUNK, :]
    qb = (q * jnp.exp2(b2)).astype(BF16)
    kd = jnp.exp2(b_last - kbb).astype(BF16)
    yield
    st = st_ref[h]
    o_inter = _dot_nt(qb, st.astype(BF16))
    upd = _dot(vbt, kd)
    yield

    def level_operands(bs, odd):
        npair = CHUNK // (2 * bs)
        r = b2.reshape(npair, 2 * bs, HGRN_DK)[:, bs:bs + 1, :]
        r = jnp.broadcast_to(r, (npair, 2 * bs, HGRN_DK)).reshape(CHUNK, HGRN_DK)
        e = jnp.exp2(jnp.where(odd, b2 - r, r - kbb))
        return (q * e).astype(BF16), e.astype(BF16)

    operands = level_operands(LEVEL_BLOCKS[0], odd_rows[0])
    st_ref[h] = st * jnp.exp2(b_last) + upd
    yield
    level_scores = []
    for nxt in range(1, len(LEVEL_BLOCKS) + 1):
        level_scores.append(_dot_nt(*operands))
        if nxt < len(LEVEL_BLOCKS):
            operands = level_operands(LEVEL_BLOCKS[nxt], odd_rows[nxt])
        yield
    nblk = CHUNK // DIAG_BLOCK
    q3 = q.reshape(nblk, DIAG_BLOCK, HGRN_DK)
    b3 = b2.reshape(nblk, DIAG_BLOCK, HGRN_DK)
    kbb3 = kbb.reshape(nblk, DIAG_BLOCK, HGRN_DK)
    s3 = jnp.zeros((nblk, DIAG_BLOCK, CHUNK), F32)
    for s in range(DIAG_BLOCK):
        tmp = q3 * jnp.exp2(b3 - kbb3[:, s:s + 1, :])
        red = jnp.sum(tmp, axis=-1, keepdims=True)
        s3 = jnp.where(lane_blk == s, red, s3)
        yield
    scores = jnp.where(diag_mask, s3.reshape(CHUNK, CHUNK), 0.0)
    for lmask, ls in zip(level_masks, level_scores):
        scores = jnp.where(lmask, ls, scores)
    scores = scores.astype(BF16)
    yield
    o_intra = _dot(scores, vb)
    yield
    o = o_inter + o_intra
    o = o * lax.rsqrt(jnp.mean(o * o, axis=-1, keepdims=True) + EPS) * gh_ref[:, vl]
    y_ref[rows, vl] = (o * g_s[slot, rows, vl]).astype(y_ref.dtype)
    yield


def _hgrn_pieces(slot, q_s, k_s, b_s, v_s, g_s, gh_ref, y_ref, st_ref):
    masks = _chunk_masks()
    for c in range(q_s.shape[1] // CHUNK):
        units = [_hgrn_unit(slot, c, h, masks, q_s, k_s, b_s, v_s, g_s, gh_ref, y_ref, st_ref)
                 for h in range(HGRN_HEADS)]
        while units:
            for u in list(units):
                try:
                    next(u)
                    yield
                except StopIteration:
                    units.remove(u)


def _inproj_tile(slot, side_work, n_side_steps, x_ref, g1_ref, wg_ref, wu_ref, wd_ref, gm_ref, win_ref, lbp_ref,
                 cw_ref, tri_ref, x1_ref, yc_ref, h_ref, acc_ref, ubuf_ref, q_s, k_s, b_s, v_s, g_s):
    tm = x_ref.shape[0]
    x = x_ref[...]
    h_ref[...] = _rms(x, g1_ref[...]).astype(BF16)
    _swiglu_into(h_ref, wg_ref, wu_ref, wd_ref, acc_ref, side_work, n_side_steps)
    x1 = x + 0.5 * acc_ref[...]
    x1_ref[...] = x1
    h_ref[...] = _rms(x1, gm_ref[...]).astype(BF16)

    def proj(p):
        return _dot(h_ref[...], win_ref[:, p * PIECE_W:(p + 1) * PIECE_W])

    lbp = lbp_ref[...]
    lbe = jnp.exp(lbp - jnp.max(lbp, axis=0, keepdims=True))
    lb = lbe[0:1, :] / jnp.sum(lbe, axis=0, keepdims=True)

    q_s[slot] = _silu(proj(0)) * (HGRN_DK ** -0.5)

    f = lb + (1.0 - lb) * jax.nn.sigmoid(proj(1))
    log2f = jnp.log(f) * LOG2_E
    hi, mid, lo = _split3(log2f)
    tri3 = tri_ref[...]
    b2 = jnp.concatenate(
        [_dot(tri3, jnp.concatenate([t[c * CHUNK:(c + 1) * CHUNK] for t in (hi, mid, lo)], axis=0))
         for c in range(tm // CHUNK)], axis=0)
    b_s[slot] = b2
    k_s[slot] = b2 - jnp.log(1.0 - f) * LOG2_E

    v_s[slot] = proj(2)
    g_s[slot] = _silu(proj(3))

    u = proj(5) * proj(6)
    ubuf_ref[SUBLANES:SUBLANES + tm, :] = u
    u1 = ubuf_ref[pl.ds(SUBLANES - 1, tm), :]
    u2 = ubuf_ref[pl.ds(SUBLANES - 2, tm), :]
    cw = cw_ref[...]
    yc_ref[...] = (proj(4) * (cw[0:1, :] * u2 + cw[1:2, :] * u1 + cw[2:3, :] * u)).astype(yc_ref.dtype)
    ubuf_ref[0:SUBLANES, :] = ubuf_ref[tm:tm + SUBLANES, :]
    for _ in side_work:
        pass


def _stage_ab_kernel(x_ref, g1_ref, wg_ref, wu_ref, wd_ref, gm_ref, win_ref, lbp_ref, cw_ref,
                     tri_ref, gh_ref,
                     x1_ref, yh_ref, yc_ref,
                     h_ref, acc_ref, ubuf_ref, q_s, k_s, b_s, v_s, g_s, st_ref,
                     *, tiles_per_seq, n_tiles):
    i = pl.program_id(0)
    wslot = i % 2
    rslot = 1 - wslot
    staging = (q_s, k_s, b_s, v_s, g_s)

    @pl.when(i == 0)
    def _():
        for ref in staging:
            ref[rslot] = jnp.zeros(ref.shape[1:], F32)

    @pl.when((i + tiles_per_seq - 1) % tiles_per_seq == 0)
    def _():
        st_ref[...] = jnp.zeros_like(st_ref)

    @pl.when(i % tiles_per_seq == 0)
    def _():
        ubuf_ref[0:SUBLANES, :] = jnp.zeros((SUBLANES, CONV_W), F32)

    @pl.when(i < n_tiles)
    def _():
        hgrn = _hgrn_pieces(rslot, *staging, gh_ref, yh_ref, st_ref)
        _inproj_tile(wslot, hgrn, (TILE_A // CHUNK) * HGRN_HEADS * HGRN_UNIT_STEPS, x_ref, g1_ref, wg_ref, wu_ref, wd_ref, gm_ref, win_ref, lbp_ref,
                     cw_ref, tri_ref, x1_ref, yc_ref, h_ref, acc_ref, ubuf_ref, *staging)

    @pl.when(i == n_tiles)
    def _():
        for _ in _hgrn_pieces(rslot, *staging, gh_ref, yh_ref, st_ref):
            pass


def _const_spec(shape):
    nd = len(shape)
    return pl.BlockSpec(shape, lambda *_: (0,) * nd, pipeline_mode=pl.Buffered(1))


def _stage_ab(x2d, g1, wg, wu, wd, gm, win, lbp, cw, tri, gh, *, seq):
    t = x2d.shape[0]
    tm = TILE_A
    n_tiles = t // tm
    cur = lambda w: pl.BlockSpec((tm, w), lambda i: (jnp.minimum(i, n_tiles - 1), 0))
    prev = lambda w: pl.BlockSpec((tm, w), lambda i: (jnp.maximum(i - 1, 0), 0))
    consts = (g1, wg, wu, wd, gm, win, lbp, cw, tri, gh)
    stage = pltpu.VMEM((2, tm, PIECE_W), F32)
    return pl.pallas_call(
        functools.partial(_stage_ab_kernel, tiles_per_seq=seq // tm, n_tiles=n_tiles),
        grid=(n_tiles + 1,),
        in_specs=[cur(D_MODEL)] + [_const_spec(c.shape) for c in consts],
        out_specs=[cur(D_MODEL), prev(HGRN_W), cur(CONV_W)],
        out_shape=[jax.ShapeDtypeStruct((t, D_MODEL), F32), jax.ShapeDtypeStruct((t, HGRN_W), BF16),
                   jax.ShapeDtypeStruct((t, CONV_W), BF16)],
        scratch_shapes=[pltpu.VMEM((tm, D_MODEL), BF16), pltpu.VMEM((tm, D_MODEL), F32),
                        pltpu.VMEM((tm + SUBLANES, CONV_W), F32),
                        stage, stage, stage, stage, stage,
                        pltpu.VMEM((HGRN_HEADS, HGRN_DV, HGRN_DK), F32)],
        compiler_params=pltpu.CompilerParams(dimension_semantics=("arbitrary",),
                                             vmem_limit_bytes=VMEM_LIMIT_BYTES),
        name="stage_ab_ffn1_inproj_hgrn2",
    )(x2d, *consts)


def _stage_kv_kernel(mem_ref, gn_ref, wkv_ref, k_ref, v_ref):
    mn = _rms(mem_ref[...], gn_ref[...]).astype(BF16)
    kv = _dot(mn, wkv_ref[...])
    k_ref[...] = kv[:, :D_MODEL].astype(BF16)
    v_ref[...] = kv[:, D_MODEL:].astype(BF16)


def _stage_kv(mem2d, gn, wkv, *, batch):
    blk = pl.BlockSpec((MEM_LEN, D_MODEL), lambda bi: (bi, 0))
    shp = jax.ShapeDtypeStruct((batch * MEM_LEN, D_MODEL), BF16)
    return pl.pallas_call(
        _stage_kv_kernel,
        grid=(batch,),
        in_specs=[blk, _const_spec(gn.shape), _const_spec(wkv.shape)],
        out_specs=[blk, blk],
        out_shape=[shp, shp],
        compiler_params=pltpu.CompilerParams(dimension_semantics=("arbitrary",),
                                             vmem_limit_bytes=VMEM_LIMIT_BYTES),
        name="stage_kv_mem",
    )(mem2d, gn, wkv)


def _stage_c_kernel(x1_ref, yh_ref, yc_ref, km_ref, vm_ref, wout_ref, gx_ref, wq_ref, wo_ref,
                    g2_ref, wg_ref, wu_ref, wd_ref, gf_ref,
                    out_ref,
                    h_ref, acc_ref, att_ref):
    x2 = (x1_ref[...]
          + _dot(yh_ref[...], wout_ref[0:HGRN_W, :])
          + _dot(yc_ref[...], wout_ref[HGRN_W:HGRN_W + CONV_W, :]))
    hq = _rms(x2, gx_ref[...]).astype(BF16)
    qm = _dot(hq, wq_ref[...]) * (MEM_HD ** -0.5)
    for h in range(MEM_HEADS):
        hs = slice(h * MEM_HD, (h + 1) * MEM_HD)
        sc = _dot_nt(qm[:, hs].astype(BF16), km_ref[:, hs])
        e = jnp.exp(sc - jnp.max(sc, axis=-1, keepdims=True))
        pv = _dot(e.astype(BF16), vm_ref[:, hs])
        att_ref[:, hs] = (pv / jnp.sum(e, axis=-1, keepdims=True)).astype(BF16)
    x3 = x2 + _dot(att_ref[...], wo_ref[...])
    h_ref[...] = _rms(x3, g2_ref[...]).astype(BF16)
    _swiglu_into(h_ref, wg_ref, wu_ref, wd_ref, acc_ref)
    x4 = x3 + 0.5 * acc_ref[...]
    out_ref[...] = _rms(x4, gf_ref[...])


def _stage_c(x1, yh, yc, km, vm, wout, gx, wq, wo, g2, wg, wu, wd, gf, *, seq):
    t = x1.shape[0]
    tm = TILE_C
    tiles_per_seq = seq // tm
    row = lambda w: pl.BlockSpec((tm, w), lambda i: (i, 0))
    memblk = pl.BlockSpec((MEM_LEN, D_MODEL), lambda i: (i // tiles_per_seq, 0))
    return pl.pallas_call(
        _stage_c_kernel,
        grid=(t // tm,),
        in_specs=[row(D_MODEL), row(HGRN_W), row(CONV_W), memblk, memblk,
                  _const_spec(wout.shape), _const_spec(gx.shape), _const_spec(wq.shape),
                  _const_spec(wo.shape), _const_spec(g2.shape), _const_spec(wg.shape),
                  _const_spec(wu.shape), _const_spec(wd.shape), _const_spec(gf.shape)],
        out_specs=row(D_MODEL),
        out_shape=jax.ShapeDtypeStruct((t, D_MODEL), F32),
        scratch_shapes=[pltpu.VMEM((tm, D_MODEL), BF16), pltpu.VMEM((tm, D_MODEL), F32),
                        pltpu.VMEM((tm, D_MODEL), BF16)],
        compiler_params=pltpu.CompilerParams(dimension_semantics=("arbitrary",),
                                             vmem_limit_bytes=VMEM_LIMIT_BYTES),
        name="stage_c_outproj_xattn_ffn2",
    )(x1, yh, yc, km, vm, wout, gx, wq, wo, g2, wg, wu, wd, gf)


CAST_BLOCK_BYTES = 4 * 1024 * 1024


def _cast_kernel(w_ref, o_ref):
    o_ref[...] = w_ref[...].astype(o_ref.dtype)


def _bf16_weight(w, l):
    _, rows, cols = w.shape
    br = max(d for d in range(16, rows + 1, 16) if rows % d == 0 and d * cols * 4 <= CAST_BLOCK_BYTES)
    return pl.pallas_call(
        _cast_kernel,
        grid=(rows // br,),
        in_specs=[pl.BlockSpec((None, br, cols), lambda i: (l, i, 0))],
        out_specs=pl.BlockSpec((br, cols), lambda i: (i, 0)),
        out_shape=jax.ShapeDtypeStruct((rows, cols), BF16),
        compiler_params=pltpu.CompilerParams(dimension_semantics=("arbitrary",),
                                             vmem_limit_bytes=VMEM_LIMIT_BYTES),
        name="cast_weight_bf16",
    )(w)


def _chunk_tri3():
    r = lax.broadcasted_iota(jnp.int32, (CHUNK, 3 * CHUNK), 0)
    c = lax.broadcasted_iota(jnp.int32, (CHUNK, 3 * CHUNK), 1)
    return (c % CHUNK <= r).astype(BF16)


def kernel(x, mem, ffn1_norm, ffn1_gate, ffn1_up, ffn1_down, mix_norm, w_in, lb_param, hgrn_out_norm, conv_w, w_out, xattn_norm, mem_norm, w_q_mem, w_kv_mem, w_o_mem, ffn2_norm, ffn2_gate, ffn2_up, ffn2_down, final_norm):
    batch, seq, _ = x.shape
    depth = ffn1_norm.shape[0]
    assert depth == 1 and seq % TILE_A == 0 and seq % TILE_C == 0
    l = 0
    x2d = x.reshape(batch * seq, D_MODEL)
    vec = lambda a: a.reshape(1, -1)

    x1, yh, yc = _stage_ab(
        x2d, vec(ffn1_norm[l]), _bf16_weight(ffn1_gate, l), _bf16_weight(ffn1_up, l),
        _bf16_weight(ffn1_down, l), vec(mix_norm[l]), _bf16_weight(w_in, l),
        lb_param, conv_w[l].T, _chunk_tri3(), vec(hgrn_out_norm[l]), seq=seq)

    km, vm = _stage_kv(mem.reshape(batch * MEM_LEN, D_MODEL), vec(mem_norm[l]),
                       _bf16_weight(w_kv_mem, l), batch=batch)

    out = _stage_c(x1, yh, yc, km, vm, _bf16_weight(w_out, l), vec(xattn_norm[l]),
                   _bf16_weight(w_q_mem, l), _bf16_weight(w_o_mem, l), vec(ffn2_norm[l]),
                   _bf16_weight(ffn2_gate, l), _bf16_weight(ffn2_up, l), _bf16_weight(ffn2_down, l),
                   vec(final_norm), seq=seq)
    return out.reshape(batch, seq, D_MODEL)
```

```python
import functools
import math

import jax
import jax.numpy as jnp
from jax import lax
from jax.experimental import pallas as pl
from jax.experimental.pallas import tpu as pltpu

F32 = jnp.float32
BF16 = jnp.bfloat16

D_MODEL = 1024
HGRN_W = 512
CONV_W = 512
HGRN_HEADS = 4
HGRN_DK = 128
HGRN_DV = 128
HGRN_F = HGRN_HEADS * HGRN_DK
CONV_K = 3
CHUNK = 64
MEM_LEN = 256
MEM_HEADS = 4
MEM_HD = D_MODEL // MEM_HEADS
D_FF = int(math.ceil(8 * D_MODEL / 3 / 256) * 256)
EPS = 1e-6
LOG2_E = 1.4426950408889634
N_IN_PIECES = 7
PIECE_W = 512

FF_CHUNK = 256
N_FF_CHUNKS = D_FF // FF_CHUNK
SUBLANES = 8
DIAG_BLOCK = SUBLANES
LEVEL_BLOCKS = (8, 16, 32)

TILE_A = 512
TILE_C = 512
VMEM_LIMIT_BYTES = 56 * 1024 * 1024


def _dot(a, b):
    return jnp.dot(a, b, preferred_element_type=F32)


def _dot_nt(a, b):
    return lax.dot_general(a, b, (((1,), (1,)), ((), ())), preferred_element_type=F32)


def _dot_tn(a, b):
    return lax.dot_general(a, b, (((0,), (0,)), ((), ())), preferred_element_type=F32)


def _rms(x, g):
    ms = jnp.mean(x * x, axis=-1, keepdims=True)
    return x * lax.rsqrt(ms + EPS) * g


def _silu(x):
    return x * jax.nn.sigmoid(x)


def _split3(x):
    hi = x.astype(BF16)
    r1 = x - hi.astype(F32)
    mid = r1.astype(BF16)
    r2 = r1 - mid.astype(F32)
    lo = r2.astype(BF16)
    return hi, mid, lo


K_TILE = 256


def _swiglu_into(h_ref, wg_ref, wu_ref, wd_ref, acc_ref, side_work=None, n_side_steps=0):
    n_slots = N_FF_CHUNKS * 3 * (D_MODEL // K_TILE)
    slot_counter = [0, 0]

    def pull():
        slot_counter[0] += 1
        due = -(-n_side_steps * slot_counter[0] // n_slots)
        while slot_counter[1] < due:
            next(side_work, None)
            slot_counter[1] += 1

    def dot_ktiles(w_ref, cols):
        out = None
        for kt in range(D_MODEL // K_TILE):
            ks = slice(kt * K_TILE, (kt + 1) * K_TILE)
            part = _dot(h_ref[:, ks], w_ref[ks, cols])
            out = part if out is None else out + part
            pull()
        return out

    for c in range(N_FF_CHUNKS):
        cols = slice(c * FF_CHUNK, (c + 1) * FF_CHUNK)
        g = dot_ktiles(wg_ref, cols)
        u = dot_ktiles(wu_ref, cols)
        a = (_silu(g) * u).astype(BF16)
        for nt in range(D_MODEL // K_TILE):
            ns = slice(nt * K_TILE, (nt + 1) * K_TILE)
            d = _dot(a, wd_ref[cols, ns])
            if c == 0:
                acc_ref[:, ns] = d
            else:
                acc_ref[:, ns] += d
            pull()


def _chunk_masks():
    row = lax.broadcasted_iota(jnp.int32, (CHUNK, CHUNK), 0)
    col = lax.broadcasted_iota(jnp.int32, (CHUNK, CHUNK), 1)
    diag_mask = (row // DIAG_BLOCK == col // DIAG_BLOCK) & (col <= row)
    rk = lax.broadcasted_iota(jnp.int32, (CHUNK, HGRN_DK), 0)
    level_masks, odd_rows = [], []
    for bs in LEVEL_BLOCKS:
        level_masks.append((row // (2 * bs) == col // (2 * bs))
                           & ((row // bs) % 2 == 1) & ((col // bs) % 2 == 0))
        odd_rows.append((rk // bs) % 2 == 1)
    nblk = CHUNK // DIAG_BLOCK
    lane = lax.broadcasted_iota(jnp.int32, (nblk, DIAG_BLOCK, CHUNK), 2)
    blk = lax.broadcasted_iota(jnp.int32, (nblk, DIAG_BLOCK, CHUNK), 0)
    lane_blk = lane - blk * DIAG_BLOCK
    return diag_mask, level_masks, odd_rows, lane_blk


HGRN_UNIT_STEPS = 6 + len(LEVEL_BLOCKS) + DIAG_BLOCK


def _hgrn_unit(slot, c, h, masks, q_s, k_s, b_s, v_s, g_s, gh_ref, y_ref, st_ref):
    diag_mask, level_masks, odd_rows, lane_blk = masks
    rows = slice(c * CHUNK, (c + 1) * CHUNK)
    lanes = slice(h * HGRN_DK, (h + 1) * HGRN_DK)
    vl = slice(h * HGRN_DV, (h + 1) * HGRN_DV)
    q = q_s[slot, rows, lanes]
    kbb = k_s[slot, rows, lanes]
    b2 = b_s[slot, rows, lanes]
    v = v_s[slot, rows, vl]
    vb = v.astype(BF16)
    vbt = v.T.astype(BF16)
    b_last = b2[CHUNK - 1:CHUNK, :]
    qb = (q * jnp.exp2(b2)).astype(BF16)
    kd = jnp.exp2(b_last - kbb).astype(BF16)
    yield
    st = st_ref[h]
    o_inter = _dot_nt(qb, st.astype(BF16))
    upd = _dot(vbt, kd)
    yield

    def level_operands(bs, odd):
        npair = CHUNK // (2 * bs)
        r = b2.reshape(npair, 2 * bs, HGRN_DK)[:, bs:bs + 1, :]
        r = jnp.broadcast_to(r, (npair, 2 * bs, HGRN_DK)).reshape(CHUNK, HGRN_DK)
        e = jnp.exp2(jnp.where(odd, b2 - r, r - kbb))
        return (q * e).astype(BF16), e.astype(BF16)

    operands = level_operands(LEVEL_BLOCKS[0], odd_rows[0])
    st_ref[h] = st * jnp.exp2(b_last) + upd
    yield
    level_scores = []
    for nxt in range(1, len(LEVEL_BLOCKS) + 1):
        level_scores.append(_dot_nt(*operands))
        if nxt < len(LEVEL_BLOCKS):
            operands = level_operands(LEVEL_BLOCKS[nxt], odd_rows[nxt])
        yield
    nblk = CHUNK // DIAG_BLOCK
    q3 = q.reshape(nblk, DIAG_BLOCK, HGRN_DK)
    b3 = b2.reshape(nblk, DIAG_BLOCK, HGRN_DK)
    kbb3 = kbb.reshape(nblk, DIAG_BLOCK, HGRN_DK)
    s3 = jnp.zeros((nblk, DIAG_BLOCK, CHUNK), F32)
    for s in range(DIAG_BLOCK):
        tmp = q3 * jnp.exp2(b3 - kbb3[:, s:s + 1, :])
        red = jnp.sum(tmp, axis=-1, keepdims=True)
        s3 = jnp.where(lane_blk == s, red, s3)
        yield
    scores = jnp.where(diag_mask, s3.reshape(CHUNK, CHUNK), 0.0)
    for lmask, ls in zip(level_masks, level_scores):
        scores = jnp.where(lmask, ls, scores)
    scores = scores.astype(BF16)
    yield
    o_intra = _dot(scores, vb)
    yield
    o = o_inter + o_intra
    o = o * lax.rsqrt(jnp.mean(o * o, axis=-1, keepdims=True) + EPS) * gh_ref[:, vl]
    y_ref[rows, vl] = (o * g_s[slot, rows, vl]).astype(y_ref.dtype)
    yield


def _hgrn_pieces(slot, q_s, k_s, b_s, v_s, g_s, gh_ref, y_ref, st_ref):
    masks = _chunk_masks()
    for c in range(q_s.shape[1] // CHUNK):
        units = [_hgrn_unit(slot, c, h, masks, q_s, k_s, b_s, v_s, g_s, gh_ref, y_ref, st_ref)
                 for h in range(HGRN_HEADS)]
        while units:
            for u in list(units):
                try:
                    next(u)
                    yield
                except StopIteration:
                    units.remove(u)


def _inproj_tile(slot, side_work, n_side_steps, x_ref, g1_ref, wg_ref, wu_ref, wd_ref, gm_ref, win_ref, lbp_ref,
                 cw_ref, tri_ref, x1_ref, yc_ref, h_ref, acc_ref, ubuf_ref, q_s, k_s, b_s, v_s, g_s):
    tm = x_ref.shape[0]
    x = x_ref[...]
    h_ref[...] = _rms(x, g1_ref[...]).astype(BF16)
    _swiglu_into(h_ref, wg_ref, wu_ref, wd_ref, acc_ref, side_work, n_side_steps)
    x1 = x + 0.5 * acc_ref[...]
    x1_ref[...] = x1
    h_ref[...] = _rms(x1, gm_ref[...]).astype(BF16)

    def proj(p):
        return _dot(h_ref[...], win_ref[:, p * PIECE_W:(p + 1) * PIECE_W])

    lbp = lbp_ref[...]
    lbe = jnp.exp(lbp - jnp.max(lbp, axis=0, keepdims=True))
    lb = lbe[0:1, :] / jnp.sum(lbe, axis=0, keepdims=True)

    q_s[slot] = _silu(proj(0)) * (HGRN_DK ** -0.5)

    f = lb + (1.0 - lb) * jax.nn.sigmoid(proj(1))
    log2f = jnp.log(f) * LOG2_E
    hi, mid, lo = _split3(log2f)
    tri3 = tri_ref[...]
    b2 = jnp.concatenate(
        [_dot(tri3, jnp.concatenate([t[c * CHUNK:(c + 1) * CHUNK] for t in (hi, mid, lo)], axis=0))
         for c in range(tm // CHUNK)], axis=0)
    b_s[slot] = b2
    k_s[slot] = b2 - jnp.log(1.0 - f) * LOG2_E

    v_s[slot] = proj(2)
    g_s[slot] = _silu(proj(3))

    u = proj(5) * proj(6)
    ubuf_ref[SUBLANES:SUBLANES + tm, :] = u
    u1 = ubuf_ref[pl.ds(SUBLANES - 1, tm), :]
    u2 = ubuf_ref[pl.ds(SUBLANES - 2, tm), :]
    cw = cw_ref[...]
    yc_ref[...] = (proj(4) * (cw[0:1, :] * u2 + cw[1:2, :] * u1 + cw[2:3, :] * u)).astype(yc_ref.dtype)
    ubuf_ref[0:SUBLANES, :] = ubuf_ref[tm:tm + SUBLANES, :]
    for _ in side_work:
        pass


def _stage_ab_kernel(x_ref, g1_ref, wg_ref, wu_ref, wd_ref, gm_ref, win_ref, lbp_ref, cw_ref,
                     tri_ref, gh_ref,
                     x1_ref, yh_ref, yc_ref,
                     h_ref, acc_ref, ubuf_ref, q_s, k_s, b_s, v_s, g_s, st_ref,
                     *, tiles_per_seq, n_tiles):
    i = pl.program_id(0)
    wslot = i % 2
    rslot = 1 - wslot
    staging = (q_s, k_s, b_s, v_s, g_s)

    @pl.when(i == 0)
    def _():
        for ref in staging:
            ref[rslot] = jnp.zeros(ref.shape[1:], F32)

    @pl.when((i + tiles_per_seq - 1) % tiles_per_seq == 0)
    def _():
        st_ref[...] = jnp.zeros_like(st_ref)

    @pl.when(i % tiles_per_seq == 0)
    def _():
        ubuf_ref[0:SUBLANES, :] = jnp.zeros((SUBLANES, CONV_W), F32)

    @pl.when(i < n_tiles)
    def _():
        hgrn = _hgrn_pieces(rslot, *staging, gh_ref, yh_ref, st_ref)
        _inproj_tile(wslot, hgrn, (TILE_A // CHUNK) * HGRN_HEADS * HGRN_UNIT_STEPS, x_ref, g1_ref, wg_ref, wu_ref, wd_ref, gm_ref, win_ref, lbp_ref,
                     cw_ref, tri_ref, x1_ref, yc_ref, h_ref, acc_ref, ubuf_ref, *staging)

    @pl.when(i == n_tiles)
    def _():
        for _ in _hgrn_pieces(rslot, *staging, gh_ref, yh_ref, st_ref):
            pass


def _const_spec(shape):
    nd = len(shape)
    return pl.BlockSpec(shape, lambda *_: (0,) * nd, pipeline_mode=pl.Buffered(1))


def _stage_ab(x2d, g1, wg, wu, wd, gm, win, lbp, cw, tri, gh, *, seq):
    t = x2d.shape[0]
    tm = TILE_A
    n_tiles = t // tm
    cur = lambda w: pl.BlockSpec((tm, w), lambda i: (jnp.minimum(i, n_tiles - 1), 0))
    prev = lambda w: pl.BlockSpec((tm, w), lambda i: (jnp.maximum(i - 1, 0), 0))
    consts = (g1, wg, wu, wd, gm, win, lbp, cw, tri, gh)
    stage = pltpu.VMEM((2, tm, PIECE_W), F32)
    return pl.pallas_call(
        functools.partial(_stage_ab_kernel, tiles_per_seq=seq // tm, n_tiles=n_tiles),
        grid=(n_tiles + 1,),
        in_specs=[cur(D_MODEL)] + [_const_spec(c.shape) for c in consts],
        out_specs=[cur(D_MODEL), prev(HGRN_W), cur(CONV_W)],
        out_shape=[jax.ShapeDtypeStruct((t, D_MODEL), F32), jax.ShapeDtypeStruct((t, HGRN_W), BF16),
                   jax.ShapeDtypeStruct((t, CONV_W), BF16)],
        scratch_shapes=[pltpu.VMEM((tm, D_MODEL), BF16), pltpu.VMEM((tm, D_MODEL), F32),
                        pltpu.VMEM((tm + SUBLANES, CONV_W), F32),
                        stage, stage, stage, stage, stage,
                        pltpu.VMEM((HGRN_HEADS, HGRN_DV, HGRN_DK), F32)],
        compiler_params=pltpu.CompilerParams(dimension_semantics=("arbitrary",),
                                             vmem_limit_bytes=VMEM_LIMIT_BYTES),
        name="stage_ab_ffn1_inproj_hgrn2",
    )(x2d, *consts)


def _stage_kv_kernel(mem_ref, gn_ref, wkv_ref, k_ref, v_ref):
    mn = _rms(mem_ref[...], gn_ref[...]).astype(BF16)
    kv = _dot(mn, wkv_ref[...])
    k_ref[...] = kv[:, :D_MODEL].astype(BF16)
    v_ref[...] = kv[:, D_MODEL:].astype(BF16)


def _stage_kv(mem2d, gn, wkv, *, batch):
    blk = pl.BlockSpec((MEM_LEN, D_MODEL), lambda bi: (bi, 0))
    shp = jax.ShapeDtypeStruct((batch * MEM_LEN, D_MODEL), BF16)
    return pl.pallas_call(
        _stage_kv_kernel,
        grid=(batch,),
        in_specs=[blk, _const_spec(gn.shape), _const_spec(wkv.shape)],
        out_specs=[blk, blk],
        out_shape=[shp, shp],
        compiler_params=pltpu.CompilerParams(dimension_semantics=("arbitrary",),
                                             vmem_limit_bytes=VMEM_LIMIT_BYTES),
        name="stage_kv_mem",
    )(mem2d, gn, wkv)


def _stage_c_kernel(x1_ref, yh_ref, yc_ref, km_ref, vm_ref, wout_ref, gx_ref, wq_ref, wo_ref,
                    g2_ref, wg_ref, wu_ref, wd_ref, gf_ref,
                    out_ref,
                    h_ref, acc_ref, att_ref):
    x2 = (x1_ref[...]
          + _dot(yh_ref[...], wout_ref[0:HGRN_W, :])
          + _dot(yc_ref[...], wout_ref[HGRN_W:HGRN_W + CONV_W, :]))
    hq = _rms(x2, gx_ref[...]).astype(BF16)
    qm = _dot(hq, wq_ref[...]) * (MEM_HD ** -0.5)
    for h in range(MEM_HEADS):
        hs = slice(h * MEM_HD, (h + 1) * MEM_HD)
        sc = _dot_nt(qm[:, hs].astype(BF16), km_ref[:, hs])
        e = jnp.exp(sc - jnp.max(sc, axis=-1, keepdims=True))
        pv = _dot(e.astype(BF16), vm_ref[:, hs])
        att_ref[:, hs] = (pv / jnp.sum(e, axis=-1, keepdims=True)).astype(BF16)
    x3 = x2 + _dot(att_ref[...], wo_ref[...])
    h_ref[...] = _rms(x3, g2_ref[...]).astype(BF16)
    _swiglu_into(h_ref, wg_ref, wu_ref, wd_ref, acc_ref)
    x4 = x3 + 0.5 * acc_ref[...]
    out_ref[...] = _rms(x4, gf_ref[...])


def _stage_c(x1, yh, yc, km, vm, wout, gx, wq, wo, g2, wg, wu, wd, gf, *, seq):
    t = x1.shape[0]
    tm = TILE_C
    tiles_per_seq = seq // tm
    row = lambda w: pl.BlockSpec((tm, w), lambda i: (i, 0))
    memblk = pl.BlockSpec((MEM_LEN, D_MODEL), lambda i: (i // tiles_per_seq, 0))
    return pl.pallas_call(
        _stage_c_kernel,
        grid=(t // tm,),
        in_specs=[row(D_MODEL), row(HGRN_W), row(CONV_W), memblk, memblk,
                  _const_spec(wout.shape), _const_spec(gx.shape), _const_spec(wq.shape),
                  _const_spec(wo.shape), _const_spec(g2.shape), _const_spec(wg.shape),
                  _const_spec(wu.shape), _const_spec(wd.shape), _const_spec(gf.shape)],
        out_specs=row(D_MODEL),
        out_shape=jax.ShapeDtypeStruct((t, D_MODEL), F32),
        scratch_shapes=[pltpu.VMEM((tm, D_MODEL), BF16), pltpu.VMEM((tm, D_MODEL), F32),
                        pltpu.VMEM((tm, D_MODEL), BF16)],
        compiler_params=pltpu.CompilerParams(dimension_semantics=("arbitrary",),
                                             vmem_limit_bytes=VMEM_LIMIT_BYTES),
        name="stage_c_outproj_xattn_ffn2",
    )(x1, yh, yc, km, vm, wout, gx, wq, wo, g2, wg, wu, wd, gf)


def _bf16_weight(w, l):
    return w[l].astype(BF16)


def _chunk_tri3():
    r = lax.broadcasted_iota(jnp.int32, (CHUNK, 3 * CHUNK), 0)
    c = lax.broadcasted_iota(jnp.int32, (CHUNK, 3 * CHUNK), 1)
    return (c % CHUNK <= r).astype(BF16)


def kernel(x, mem, ffn1_norm, ffn1_gate, ffn1_up, ffn1_down, mix_norm, w_in, lb_param, hgrn_out_norm, conv_w, w_out, xattn_norm, mem_norm, w_q_mem, w_kv_mem, w_o_mem, ffn2_norm, ffn2_gate, ffn2_up, ffn2_down, final_norm):
    batch, seq, _ = x.shape
    depth = ffn1_norm.shape[0]
    assert depth == 1 and seq % TILE_A == 0 and seq % TILE_C == 0
    l = 0
    x2d = x.reshape(batch * seq, D_MODEL)
    vec = lambda a: a.reshape(1, -1)

    x1, yh, yc = _stage_ab(
        x2d, vec(ffn1_norm[l]), _bf16_weight(ffn1_gate, l), _bf16_weight(ffn1_up, l),
        _bf16_weight(ffn1_down, l), vec(mix_norm[l]), _bf16_weight(w_in, l),
        lb_param, conv_w[l].T, _chunk_tri3(), vec(hgrn_out_norm[l]), seq=seq)

    km, vm = _stage_kv(mem.reshape(batch * MEM_LEN, D_MODEL), vec(mem_norm[l]),
                       _bf16_weight(w_kv_mem, l), batch=batch)

    out = _stage_c(x1, yh, yc, km, vm, _bf16_weight(w_out, l), vec(xattn_norm[l]),
                   _bf16_weight(w_q_mem, l), _bf16_weight(w_o_mem, l), vec(ffn2_norm[l]),
                   _bf16_weight(ffn2_gate, l), _bf16_weight(ffn2_up, l), _bf16_weight(ffn2_down, l),
                   vec(final_norm), seq=seq)
    return out.reshape(batch, seq, D_MODEL)
```

```python
import functools
import math

import jax
import jax.numpy as jnp
from jax import lax
from jax.experimental import pallas as pl
from jax.experimental.pallas import tpu as pltpu

F32 = jnp.float32
BF16 = jnp.bfloat16

D_MODEL = 1024
HGRN_W = 512
CONV_W = 512
HGRN_HEADS = 4
HGRN_DK = 128
HGRN_DV = 128
HGRN_F = HGRN_HEADS * HGRN_DK
CONV_K = 3
CHUNK = 64
MEM_LEN = 256
MEM_HEADS = 4
MEM_HD = D_MODEL // MEM_HEADS
D_FF = int(math.ceil(8 * D_MODEL / 3 / 256) * 256)
EPS = 1e-6
LOG2_E = 1.4426950408889634
N_IN_PIECES = 7
PIECE_W = 512

FF_CHUNK = 256
N_FF_CHUNKS = D_FF // FF_CHUNK
SUBLANES = 8
DIAG_BLOCK = SUBLANES
LEVEL_BLOCKS = (8, 16, 32)

CAST_STEPS = 16
TILE_A = 512
TILE_C = 512
VMEM_LIMIT_BYTES = 60 * 1024 * 1024


def _dot(a, b):
    return jnp.dot(a, b, preferred_element_type=F32)


def _dot_nt(a, b):
    return lax.dot_general(a, b, (((1,), (1,)), ((), ())), preferred_element_type=F32)


def _dot_tn(a, b):
    return lax.dot_general(a, b, (((0,), (0,)), ((), ())), preferred_element_type=F32)


def _rms(x, g):
    ms = jnp.mean(x * x, axis=-1, keepdims=True)
    return x * lax.rsqrt(ms + EPS) * g


def _silu(x):
    return x * jax.nn.sigmoid(x)


def _split3(x):
    hi = x.astype(BF16)
    r1 = x - hi.astype(F32)
    mid = r1.astype(BF16)
    r2 = r1 - mid.astype(F32)
    lo = r2.astype(BF16)
    return hi, mid, lo


K_TILE = 256


def _swiglu_into(h_ref, wg_ref, wu_ref, wd_ref, acc_ref, side_work=None, n_side_steps=0):
    n_slots = N_FF_CHUNKS * 3 * (D_MODEL // K_TILE)
    slot_counter = [0, 0]

    def pull():
        slot_counter[0] += 1
        due = -(-n_side_steps * slot_counter[0] // n_slots)
        while slot_counter[1] < due:
            next(side_work, None)
            slot_counter[1] += 1

    def dot_ktiles(w_ref, cols):
        out = None
        for kt in range(D_MODEL // K_TILE):
            ks = slice(kt * K_TILE, (kt + 1) * K_TILE)
            part = _dot(h_ref[:, ks], w_ref[ks, cols])
            out = part if out is None else out + part
            pull()
        return out

    for c in range(N_FF_CHUNKS):
        cols = slice(c * FF_CHUNK, (c + 1) * FF_CHUNK)
        g = dot_ktiles(wg_ref, cols)
        u = dot_ktiles(wu_ref, cols)
        a = (_silu(g) * u).astype(BF16)
        for nt in range(D_MODEL // K_TILE):
            ns = slice(nt * K_TILE, (nt + 1) * K_TILE)
            d = _dot(a, wd_ref[cols, ns])
            if c == 0:
                acc_ref[:, ns] = d
            else:
                acc_ref[:, ns] += d
            pull()


def _chunk_masks():
    row = lax.broadcasted_iota(jnp.int32, (CHUNK, CHUNK), 0)
    col = lax.broadcasted_iota(jnp.int32, (CHUNK, CHUNK), 1)
    diag_mask = (row // DIAG_BLOCK == col // DIAG_BLOCK) & (col <= row)
    rk = lax.broadcasted_iota(jnp.int32, (CHUNK, HGRN_DK), 0)
    level_masks, odd_rows = [], []
    for bs in LEVEL_BLOCKS:
        level_masks.append((row // (2 * bs) == col // (2 * bs))
                           & ((row // bs) % 2 == 1) & ((col // bs) % 2 == 0))
        odd_rows.append((rk // bs) % 2 == 1)
    nblk = CHUNK // DIAG_BLOCK
    lane = lax.broadcasted_iota(jnp.int32, (nblk, DIAG_BLOCK, CHUNK), 2)
    blk = lax.broadcasted_iota(jnp.int32, (nblk, DIAG_BLOCK, CHUNK), 0)
    lane_blk = lane - blk * DIAG_BLOCK
    return diag_mask, level_masks, odd_rows, lane_blk


HGRN_UNIT_STEPS = 6 + len(LEVEL_BLOCKS) + DIAG_BLOCK


def _hgrn_unit(slot, c, h, masks, q_s, k_s, b_s, v_s, g_s, gh_ref, y_ref, st_ref):
    diag_mask, level_masks, odd_rows, lane_blk = masks
    rows = slice(c * CHUNK, (c + 1) * CHUNK)
    lanes = slice(h * HGRN_DK, (h + 1) * HGRN_DK)
    vl = slice(h * HGRN_DV, (h + 1) * HGRN_DV)
    q = q_s[slot, rows, lanes]
    kbb = k_s[slot, rows, lanes]
    b2 = b_s[slot, rows, lanes]
    v = v_s[slot, rows, vl]
    vb = v.astype(BF16)
    vbt = v.T.astype(BF16)
    b_last = b2[CHUNK - 1:CHUNK, :]
    qb = (q * jnp.exp2(b2)).astype(BF16)
    kd = jnp.exp2(b_last - kbb).astype(BF16)
    yield
    st = st_ref[h]
    o_inter = _dot_nt(qb, st.astype(BF16))
    upd = _dot(vbt, kd)
    yield

    def level_operands(bs, odd):
        npair = CHUNK // (2 * bs)
        r = b2.reshape(npair, 2 * bs, HGRN_DK)[:, bs:bs + 1, :]
        r = jnp.broadcast_to(r, (npair, 2 * bs, HGRN_DK)).reshape(CHUNK, HGRN_DK)
        e = jnp.exp2(jnp.where(odd, b2 - r, r - kbb))
        return (q * e).astype(BF16), e.astype(BF16)

    operands = level_operands(LEVEL_BLOCKS[0], odd_rows[0])
    st_ref[h] = st * jnp.exp2(b_last) + upd
    yield
    level_scores = []
    for nxt in range(1, len(LEVEL_BLOCKS) + 1):
        level_scores.append(_dot_nt(*operands))
        if nxt < len(LEVEL_BLOCKS):
            operands = level_operands(LEVEL_BLOCKS[nxt], odd_rows[nxt])
        yield
    nblk = CHUNK // DIAG_BLOCK
    q3 = q.reshape(nblk, DIAG_BLOCK, HGRN_DK)
    b3 = b2.reshape(nblk, DIAG_BLOCK, HGRN_DK)
    kbb3 = kbb.reshape(nblk, DIAG_BLOCK, HGRN_DK)
    s3 = jnp.zeros((nblk, DIAG_BLOCK, CHUNK), F32)
    for s in range(DIAG_BLOCK):
        tmp = q3 * jnp.exp2(b3 - kbb3[:, s:s + 1, :])
        red = jnp.sum(tmp, axis=-1, keepdims=True)
        s3 = jnp.where(lane_blk == s, red, s3)
        yield
    scores = jnp.where(diag_mask, s3.reshape(CHUNK, CHUNK), 0.0)
    for lmask, ls in zip(level_masks, level_scores):
        scores = jnp.where(lmask, ls, scores)
    scores = scores.astype(BF16)
    yield
    o_intra = _dot(scores, vb)
    yield
    o = o_inter + o_intra
    o = o * lax.rsqrt(jnp.mean(o * o, axis=-1, keepdims=True) + EPS) * gh_ref[:, vl]
    y_ref[rows, vl] = (o * g_s[slot, rows, vl]).astype(y_ref.dtype)
    yield


def _hgrn_pieces(slot, q_s, k_s, b_s, v_s, g_s, gh_ref, y_ref, st_ref):
    masks = _chunk_masks()
    for c in range(q_s.shape[1] // CHUNK):
        units = [_hgrn_unit(slot, c, h, masks, q_s, k_s, b_s, v_s, g_s, gh_ref, y_ref, st_ref)
                 for h in range(HGRN_HEADS)]
        while units:
            for u in list(units):
                try:
                    next(u)
                    yield
                except StopIteration:
                    units.remove(u)


def _cast_weight_rows(i, pairs):
    for w_ref, dst_ref in pairs:
        rb = w_ref.shape[0]
        dst_ref[pl.ds(pl.multiple_of(i * rb, rb), rb), :] = w_ref[...].astype(dst_ref.dtype)


def _weight_block_spec(w, l):
    _, rows, cols = w.shape
    return pl.BlockSpec((None, rows // CAST_STEPS, cols),
                        lambda i: (l, jnp.minimum(i, CAST_STEPS - 1), 0))


def _weight_scratch(w):
    return pltpu.VMEM(w.shape[1:], BF16)


def _inproj_tile(slot, side_work, n_side_steps, x_ref, g1_ref, wg_ref, wu_ref, wd_ref, gm_ref, win_ref, lbp_ref,
                 cw_ref, tri_ref, x1_ref, yc_ref, h_ref, acc_ref, ubuf_ref, q_s, k_s, b_s, v_s, g_s):
    tm = x_ref.shape[0]
    x = x_ref[...]
    h_ref[...] = _rms(x, g1_ref[...]).astype(BF16)
    _swiglu_into(h_ref, wg_ref, wu_ref, wd_ref, acc_ref, side_work, n_side_steps)
    x1 = x + 0.5 * acc_ref[...]
    x1_ref[...] = x1
    h_ref[...] = _rms(x1, gm_ref[...]).astype(BF16)

    def proj(p):
        return _dot(h_ref[...], win_ref[:, p * PIECE_W:(p + 1) * PIECE_W])

    lbp = lbp_ref[...]
    lbe = jnp.exp(lbp - jnp.max(lbp, axis=0, keepdims=True))
    lb = lbe[0:1, :] / jnp.sum(lbe, axis=0, keepdims=True)

    q_s[slot] = _silu(proj(0)) * (HGRN_DK ** -0.5)

    f = lb + (1.0 - lb) * jax.nn.sigmoid(proj(1))
    log2f = jnp.log(f) * LOG2_E
    hi, mid, lo = _split3(log2f)
    tri3 = tri_ref[...]
    b2 = jnp.concatenate(
        [_dot(tri3, jnp.concatenate([t[c * CHUNK:(c + 1) * CHUNK] for t in (hi, mid, lo)], axis=0))
         for c in range(tm // CHUNK)], axis=0)
    b_s[slot] = b2
    k_s[slot] = b2 - jnp.log(1.0 - f) * LOG2_E

    v_s[slot] = proj(2)
    g_s[slot] = _silu(proj(3))

    u = proj(5) * proj(6)
    ubuf_ref[SUBLANES:SUBLANES + tm, :] = u
    u1 = ubuf_ref[pl.ds(SUBLANES - 1, tm), :]
    u2 = ubuf_ref[pl.ds(SUBLANES - 2, tm), :]
    cw = cw_ref[...]
    yc_ref[...] = (proj(4) * (cw[0:1, :] * u2 + cw[1:2, :] * u1 + cw[2:3, :] * u)).astype(yc_ref.dtype)
    ubuf_ref[0:SUBLANES, :] = ubuf_ref[tm:tm + SUBLANES, :]
    for _ in side_work:
        pass


def _stage_ab_kernel(x_ref, g1_ref, wg_f32, wu_f32, wd_f32, gm_ref, win_f32, lbp_ref, cw_ref,
                     tri_ref, gh_ref,
                     x1_ref, yh_ref, yc_ref,
                     wg_ref, wu_ref, wd_ref, win_ref,
                     h_ref, acc_ref, ubuf_ref, q_s, k_s, b_s, v_s, g_s, st_ref,
                     *, tiles_per_seq, n_tiles):
    i = pl.program_id(0)
    j = i - CAST_STEPS
    wslot = j % 2
    rslot = 1 - wslot
    staging = (q_s, k_s, b_s, v_s, g_s)

    @pl.when(i < CAST_STEPS)
    def _():
        _cast_weight_rows(i, ((wg_f32, wg_ref), (wu_f32, wu_ref), (wd_f32, wd_ref), (win_f32, win_ref)))

    @pl.when(j == 0)
    def _():
        for ref in staging:
            ref[rslot] = jnp.zeros(ref.shape[1:], F32)

    @pl.when((j + tiles_per_seq - 1) % tiles_per_seq == 0)
    def _():
        st_ref[...] = jnp.zeros_like(st_ref)

    @pl.when(j % tiles_per_seq == 0)
    def _():
        ubuf_ref[0:SUBLANES, :] = jnp.zeros((SUBLANES, CONV_W), F32)

    @pl.when((j >= 0) & (j < n_tiles))
    def _():
        hgrn = _hgrn_pieces(rslot, *staging, gh_ref, yh_ref, st_ref)
        _inproj_tile(wslot, hgrn, (TILE_A // CHUNK) * HGRN_HEADS * HGRN_UNIT_STEPS, x_ref, g1_ref, wg_ref, wu_ref, wd_ref, gm_ref, win_ref, lbp_ref,
                     cw_ref, tri_ref, x1_ref, yc_ref, h_ref, acc_ref, ubuf_ref, *staging)

    @pl.when(j == n_tiles)
    def _():
        for _ in _hgrn_pieces(rslot, *staging, gh_ref, yh_ref, st_ref):
            pass


def _const_spec(shape):
    nd = len(shape)
    return pl.BlockSpec(shape, lambda *_: (0,) * nd, pipeline_mode=pl.Buffered(1))


def _stage_ab(x2d, g1, wg, wu, wd, gm, win, lbp, cw, tri, gh, *, l, seq):
    t = x2d.shape[0]
    tm = TILE_A
    n_tiles = t // tm
    tile = lambda i: jnp.clip(i - CAST_STEPS, 0, n_tiles - 1)
    cur = lambda w: pl.BlockSpec((tm, w), lambda i: (tile(i), 0))
    prev = lambda w: pl.BlockSpec((tm, w), lambda i: (tile(i - 1), 0))
    wspec = lambda w: _weight_block_spec(w, l)
    cspec = lambda c: _const_spec(c.shape)
    stage = pltpu.VMEM((2, tm, PIECE_W), F32)
    return pl.pallas_call(
        functools.partial(_stage_ab_kernel, tiles_per_seq=seq // tm, n_tiles=n_tiles),
        grid=(CAST_STEPS + n_tiles + 1,),
        in_specs=[cur(D_MODEL), cspec(g1), wspec(wg), wspec(wu), wspec(wd), cspec(gm), wspec(win),
                  cspec(lbp), cspec(cw), cspec(tri), cspec(gh)],
        out_specs=[cur(D_MODEL), prev(HGRN_W), cur(CONV_W)],
        out_shape=[jax.ShapeDtypeStruct((t, D_MODEL), F32), jax.ShapeDtypeStruct((t, HGRN_W), BF16),
                   jax.ShapeDtypeStruct((t, CONV_W), BF16)],
        scratch_shapes=[_weight_scratch(wg), _weight_scratch(wu), _weight_scratch(wd), _weight_scratch(win),
                        pltpu.VMEM((tm, D_MODEL), BF16), pltpu.VMEM((tm, D_MODEL), F32),
                        pltpu.VMEM((tm + SUBLANES, CONV_W), F32),
                        stage, stage, stage, stage, stage,
                        pltpu.VMEM((HGRN_HEADS, HGRN_DV, HGRN_DK), F32)],
        compiler_params=pltpu.CompilerParams(dimension_semantics=("arbitrary",),
                                             vmem_limit_bytes=VMEM_LIMIT_BYTES),
        name="stage_ab_ffn1_inproj_hgrn2",
    )(x2d, g1, wg, wu, wd, gm, win, lbp, cw, tri, gh)


def _stage_c_kernel(x1_ref, yh_ref, yc_ref, mem_ref, gmem_ref, wout_f32, gx_ref, wq_f32, wo_f32, wkv_f32,
                    g2_ref, wg_f32, wu_f32, wd_f32, gf_ref,
                    out_ref,
                    wout_ref, wq_ref, wo_ref, wkv_ref, wg_ref, wu_ref, wd_ref,
                    km_ref, vm_ref, h_ref, acc_ref, att_ref, *, tiles_per_seq):
    i = pl.program_id(0)
    j = i - CAST_STEPS

    @pl.when(i < CAST_STEPS)
    def _():
        _cast_weight_rows(i, ((wout_f32, wout_ref), (wq_f32, wq_ref), (wo_f32, wo_ref), (wkv_f32, wkv_ref),
                              (wg_f32, wg_ref), (wu_f32, wu_ref), (wd_f32, wd_ref)))

    @pl.when((j >= 0) & (j % tiles_per_seq == 0))
    def _():
        mn = _rms(mem_ref[...], gmem_ref[...]).astype(BF16)
        kv = _dot(mn, wkv_ref[...])
        km_ref[...] = kv[:, :D_MODEL].astype(BF16)
        vm_ref[...] = kv[:, D_MODEL:].astype(BF16)

    @pl.when(j >= 0)
    def _():
        x2 = (x1_ref[...]
              + _dot(yh_ref[...], wout_ref[0:HGRN_W, :])
              + _dot(yc_ref[...], wout_ref[HGRN_W:HGRN_W + CONV_W, :]))
        hq = _rms(x2, gx_ref[...]).astype(BF16)
        qm = _dot(hq, wq_ref[...]) * (MEM_HD ** -0.5)
        for h in range(MEM_HEADS):
            hs = slice(h * MEM_HD, (h + 1) * MEM_HD)
            sc = _dot_nt(qm[:, hs].astype(BF16), km_ref[:, hs])
            e = jnp.exp(sc - jnp.max(sc, axis=-1, keepdims=True))
            pv = _dot(e.astype(BF16), vm_ref[:, hs])
            att_ref[:, hs] = (pv / jnp.sum(e, axis=-1, keepdims=True)).astype(BF16)
        x3 = x2 + _dot(att_ref[...], wo_ref[...])
        h_ref[...] = _rms(x3, g2_ref[...]).astype(BF16)
        _swiglu_into(h_ref, wg_ref, wu_ref, wd_ref, acc_ref)
        x4 = x3 + 0.5 * acc_ref[...]
        out_ref[...] = _rms(x4, gf_ref[...])


def _stage_c(x1, yh, yc, mem2d, gmem, wout, gx, wq, wo, wkv, g2, wg, wu, wd, gf, *, l, seq):
    t = x1.shape[0]
    tm = TILE_C
    tiles_per_seq = seq // tm
    tile = lambda i: jnp.maximum(i - CAST_STEPS, 0)
    row = lambda w: pl.BlockSpec((tm, w), lambda i: (tile(i), 0))
    memblk = pl.BlockSpec((MEM_LEN, D_MODEL), lambda i: (tile(i) // tiles_per_seq, 0))
    wspec = lambda w: _weight_block_spec(w, l)
    cspec = lambda c: _const_spec(c.shape)
    weights = (wout, wq, wo, wkv, wg, wu, wd)
    return pl.pallas_call(
        functools.partial(_stage_c_kernel, tiles_per_seq=tiles_per_seq),
        grid=(CAST_STEPS + t // tm,),
        in_specs=[row(D_MODEL), row(HGRN_W), row(CONV_W), memblk, cspec(gmem), wspec(wout), cspec(gx),
                  wspec(wq), wspec(wo), wspec(wkv), cspec(g2), wspec(wg), wspec(wu), wspec(wd), cspec(gf)],
        out_specs=row(D_MODEL),
        out_shape=jax.ShapeDtypeStruct((t, D_MODEL), F32),
        scratch_shapes=[_weight_scratch(w) for w in weights] + [
            pltpu.VMEM((MEM_LEN, D_MODEL), BF16), pltpu.VMEM((MEM_LEN, D_MODEL), BF16),
            pltpu.VMEM((tm, D_MODEL), BF16), pltpu.VMEM((tm, D_MODEL), F32),
            pltpu.VMEM((tm, D_MODEL), BF16)],
        compiler_params=pltpu.CompilerParams(dimension_semantics=("arbitrary",),
                                             vmem_limit_bytes=VMEM_LIMIT_BYTES),
        name="stage_c_outproj_xattn_ffn2",
    )(x1, yh, yc, mem2d, gmem, wout, gx, wq, wo, wkv, g2, wg, wu, wd, gf)


def _chunk_tri3():
    r = lax.broadcasted_iota(jnp.int32, (CHUNK, 3 * CHUNK), 0)
    c = lax.broadcasted_iota(jnp.int32, (CHUNK, 3 * CHUNK), 1)
    return (c % CHUNK <= r).astype(BF16)


def kernel(x, mem, ffn1_norm, ffn1_gate, ffn1_up, ffn1_down, mix_norm, w_in, lb_param, hgrn_out_norm, conv_w, w_out, xattn_norm, mem_norm, w_q_mem, w_kv_mem, w_o_mem, ffn2_norm, ffn2_gate, ffn2_up, ffn2_down, final_norm):
    batch, seq, _ = x.shape
    depth = ffn1_norm.shape[0]
    assert depth == 1 and seq % TILE_A == 0 and seq % TILE_C == 0
    l = 0
    x2d = x.reshape(batch * seq, D_MODEL)
    vec = lambda a: a.reshape(1, -1)

    x1, yh, yc = _stage_ab(
        x2d, vec(ffn1_norm[l]), ffn1_gate, ffn1_up, ffn1_down, vec(mix_norm[l]), w_in,
        lb_param, conv_w[l].T, _chunk_tri3(), vec(hgrn_out_norm[l]), l=l, seq=seq)

    out = _stage_c(x1, yh, yc, mem.reshape(batch * MEM_LEN, D_MODEL), vec(mem_norm[l]), w_out,
                   vec(xattn_norm[l]), w_q_mem, w_o_mem, w_kv_mem, vec(ffn2_norm[l]),
                   ffn2_gate, ffn2_up, ffn2_down, vec(final_norm), l=l, seq=seq)
    return out.reshape(batch, seq, D_MODEL)
```

```python
import functools
import math

import jax
import jax.numpy as jnp
from jax import lax
from jax.experimental import pallas as pl
from jax.experimental.pallas import tpu as pltpu

F32 = jnp.float32
BF16 = jnp.bfloat16

D_MODEL = 1024
HGRN_W = 512
CONV_W = 512
HGRN_HEADS = 4
HGRN_DK = 128
HGRN_DV = 128
HGRN_F = HGRN_HEADS * HGRN_DK
CONV_K = 3
CHUNK = 64
MEM_LEN = 256
MEM_HEADS = 4
MEM_HD = D_MODEL // MEM_HEADS
D_FF = int(math.ceil(8 * D_MODEL / 3 / 256) * 256)
EPS = 1e-6
LOG2_E = 1.4426950408889634
N_IN_PIECES = 7
PIECE_W = 512

FF_CHUNK = 256
N_FF_CHUNKS = D_FF // FF_CHUNK
SUBLANES = 8
DIAG_BLOCK = SUBLANES
LEVEL_BLOCKS = (8, 16, 32)

CAST_STEPS = 16
TILE_A = 512
TILE_C = 512
VMEM_LIMIT_BYTES = 60 * 1024 * 1024


def _dot(a, b):
    return jnp.dot(a, b, preferred_element_type=F32)


def _dot_nt(a, b):
    return lax.dot_general(a, b, (((1,), (1,)), ((), ())), preferred_element_type=F32)


def _rms(x, g):
    ms = jnp.mean(x * x, axis=-1, keepdims=True)
    return x * lax.rsqrt(ms + EPS) * g


def _silu(x):
    return x * jax.nn.sigmoid(x)


def _split3(x):
    hi = x.astype(BF16)
    r1 = x - hi.astype(F32)
    mid = r1.astype(BF16)
    r2 = r1 - mid.astype(F32)
    lo = r2.astype(BF16)
    return hi, mid, lo


SIDE_GROUP = 12
K_TILE = 256


def _swiglu_into(h_ref, wg_ref, wu_ref, wd_ref, acc_ref, side_work=None, n_side_steps=0):
    n_slots = N_FF_CHUNKS * 3 * (D_MODEL // K_TILE)
    slot_counter = [0, 0]

    def pull():
        slot_counter[0] += 1
        if slot_counter[0] % SIDE_GROUP:
            return
        due = -(-n_side_steps * slot_counter[0] // n_slots)
        while slot_counter[1] < due:
            next(side_work, None)
            slot_counter[1] += 1

    def dot_ktiles(w_ref, cols):
        out = None
        for kt in range(D_MODEL // K_TILE):
            ks = slice(kt * K_TILE, (kt + 1) * K_TILE)
            part = _dot(h_ref[:, ks], w_ref[ks, cols])
            out = part if out is None else out + part
            pull()
        return out

    for c in range(N_FF_CHUNKS):
        cols = slice(c * FF_CHUNK, (c + 1) * FF_CHUNK)
        g = dot_ktiles(wg_ref, cols)
        u = dot_ktiles(wu_ref, cols)
        a = (_silu(g) * u).astype(BF16)
        for nt in range(D_MODEL // K_TILE):
            ns = slice(nt * K_TILE, (nt + 1) * K_TILE)
            d = _dot(a, wd_ref[cols, ns])
            if c == 0:
                acc_ref[:, ns] = d
            else:
                acc_ref[:, ns] += d
            pull()


def _chunk_masks():
    row = lax.broadcasted_iota(jnp.int32, (CHUNK, CHUNK), 0)
    col = lax.broadcasted_iota(jnp.int32, (CHUNK, CHUNK), 1)
    diag_mask = (row // DIAG_BLOCK == col // DIAG_BLOCK) & (col <= row)
    rk = lax.broadcasted_iota(jnp.int32, (CHUNK, HGRN_DK), 0)
    level_masks, odd_rows = [], []
    for bs in LEVEL_BLOCKS:
        level_masks.append((row // (2 * bs) == col // (2 * bs))
                           & ((row // bs) % 2 == 1) & ((col // bs) % 2 == 0))
        odd_rows.append((rk // bs) % 2 == 1)
    nblk = CHUNK // DIAG_BLOCK
    lane = lax.broadcasted_iota(jnp.int32, (nblk, DIAG_BLOCK, CHUNK), 2)
    blk = lax.broadcasted_iota(jnp.int32, (nblk, DIAG_BLOCK, CHUNK), 0)
    lane_blk = lane - blk * DIAG_BLOCK
    return diag_mask, level_masks, odd_rows, lane_blk


HGRN_UNIT_STEPS = 6 + len(LEVEL_BLOCKS) + DIAG_BLOCK


def _hgrn_unit(slot, c, h, masks, q_s, k_s, b_s, v_s, g_s, gh_ref, y_ref, st_ref):
    diag_mask, level_masks, odd_rows, lane_blk = masks
    rows = slice(c * CHUNK, (c + 1) * CHUNK)
    lanes = slice(h * HGRN_DK, (h + 1) * HGRN_DK)
    vl = slice(h * HGRN_DV, (h + 1) * HGRN_DV)
    q = q_s[slot, rows, lanes]
    kbb = k_s[slot, rows, lanes]
    b2 = b_s[slot, rows, lanes]
    v = v_s[slot, rows, vl]
    vb = v.astype(BF16)
    vbt = v.T.astype(BF16)
    b_last = b2[CHUNK - 1:CHUNK, :]
    qb = (q * jnp.exp2(b2)).astype(BF16)
    kd = jnp.exp2(b_last - kbb).astype(BF16)
    yield
    st = st_ref[h]
    o_inter = _dot_nt(qb, st.astype(BF16))
    upd = _dot(vbt, kd)
    yield

    def level_operands(bs, odd):
        npair = CHUNK // (2 * bs)
        r = b2.reshape(npair, 2 * bs, HGRN_DK)[:, bs:bs + 1, :]
        r = jnp.broadcast_to(r, (npair, 2 * bs, HGRN_DK)).reshape(CHUNK, HGRN_DK)
        e = jnp.exp2(jnp.where(odd, b2 - r, r - kbb))
        return (q * e).astype(BF16), e.astype(BF16)

    operands = level_operands(LEVEL_BLOCKS[0], odd_rows[0])
    st_ref[h] = st * jnp.exp2(b_last) + upd
    yield
    level_scores = []
    for nxt in range(1, len(LEVEL_BLOCKS) + 1):
        level_scores.append(_dot_nt(*operands))
        if nxt < len(LEVEL_BLOCKS):
            operands = level_operands(LEVEL_BLOCKS[nxt], odd_rows[nxt])
        yield
    nblk = CHUNK // DIAG_BLOCK
    q3 = q.reshape(nblk, DIAG_BLOCK, HGRN_DK)
    b3 = b2.reshape(nblk, DIAG_BLOCK, HGRN_DK)
    kbb3 = kbb.reshape(nblk, DIAG_BLOCK, HGRN_DK)
    s3 = jnp.zeros((nblk, DIAG_BLOCK, CHUNK), F32)
    for s in range(DIAG_BLOCK):
        tmp = q3 * jnp.exp2(b3 - kbb3[:, s:s + 1, :])
        red = jnp.sum(tmp, axis=-1, keepdims=True)
        s3 = jnp.where(lane_blk == s, red, s3)
        yield
    scores = jnp.where(diag_mask, s3.reshape(CHUNK, CHUNK), 0.0)
    for lmask, ls in zip(level_masks, level_scores):
        scores = jnp.where(lmask, ls, scores)
    scores = scores.astype(BF16)
    yield
    o_intra = _dot(scores, vb)
    yield
    o = o_inter + o_intra
    o = o * lax.rsqrt(jnp.mean(o * o, axis=-1, keepdims=True) + EPS) * gh_ref[:, vl]
    y_ref[rows, vl] = (o * g_s[slot, rows, vl]).astype(y_ref.dtype)
    yield


def _hgrn_pieces(slot, q_s, k_s, b_s, v_s, g_s, gh_ref, y_ref, st_ref):
    masks = _chunk_masks()
    for c in range(q_s.shape[1] // CHUNK):
        units = [_hgrn_unit(slot, c, h, masks, q_s, k_s, b_s, v_s, g_s, gh_ref, y_ref, st_ref)
                 for h in range(HGRN_HEADS)]
        while units:
            for u in list(units):
                try:
                    next(u)
                    yield
                except StopIteration:
                    units.remove(u)


def _cast_weight_rows(i, pairs):
    for w_ref, dst_ref in pairs:
        rb = w_ref.shape[0]
        dst_ref[pl.ds(pl.multiple_of(i * rb, rb), rb), :] = w_ref[...].astype(dst_ref.dtype)


def _weight_block_spec(w, l):
    _, rows, cols = w.shape
    return pl.BlockSpec((None, rows // CAST_STEPS, cols),
                        lambda i: (l, jnp.minimum(i, CAST_STEPS - 1), 0))


def _weight_scratch(w):
    return pltpu.VMEM(w.shape[1:], BF16)


def _inproj_tile(slot, side_work, n_side_steps, x_ref, g1_ref, wg_ref, wu_ref, wd_ref, gm_ref, win_ref, lbp_ref,
                 cw_ref, tri_ref, x1_ref, yc_ref, h_ref, acc_ref, ubuf_ref, q_s, k_s, b_s, v_s, g_s):
    tm = x_ref.shape[0]
    x = x_ref[...]
    h_ref[...] = _rms(x, g1_ref[...]).astype(BF16)
    _swiglu_into(h_ref, wg_ref, wu_ref, wd_ref, acc_ref, side_work, n_side_steps)
    x1 = x + 0.5 * acc_ref[...]
    x1_ref[...] = x1
    h_ref[...] = _rms(x1, gm_ref[...]).astype(BF16)

    def proj(p):
        return _dot(h_ref[...], win_ref[:, p * PIECE_W:(p + 1) * PIECE_W])

    lbp = lbp_ref[...]
    lbe = jnp.exp(lbp - jnp.max(lbp, axis=0, keepdims=True))
    lb = lbe[0:1, :] / jnp.sum(lbe, axis=0, keepdims=True)

    q_s[slot] = _silu(proj(0)) * (HGRN_DK ** -0.5)

    f = lb + (1.0 - lb) * jax.nn.sigmoid(proj(1))
    log2f = jnp.log(f) * LOG2_E
    hi, mid, lo = _split3(log2f)
    tri3 = tri_ref[...]
    b2 = jnp.concatenate(
        [_dot(tri3, jnp.concatenate([t[c * CHUNK:(c + 1) * CHUNK] for t in (hi, mid, lo)], axis=0))
         for c in range(tm // CHUNK)], axis=0)
    b_s[slot] = b2
    k_s[slot] = b2 - jnp.log(1.0 - f) * LOG2_E

    v_s[slot] = proj(2)
    g_s[slot] = _silu(proj(3))

    u = proj(5) * proj(6)
    ubuf_ref[SUBLANES:SUBLANES + tm, :] = u
    u1 = ubuf_ref[pl.ds(SUBLANES - 1, tm), :]
    u2 = ubuf_ref[pl.ds(SUBLANES - 2, tm), :]
    cw = cw_ref[...]
    yc_ref[...] = (proj(4) * (cw[0:1, :] * u2 + cw[1:2, :] * u1 + cw[2:3, :] * u)).astype(yc_ref.dtype)
    ubuf_ref[0:SUBLANES, :] = ubuf_ref[tm:tm + SUBLANES, :]
    for _ in side_work:
        pass


def _stage_ab_kernel(x_ref, g1_ref, wg_f32, wu_f32, wd_f32, gm_ref, win_f32, lbp_ref, cw_ref,
                     tri_ref, gh_ref,
                     x1_ref, yh_ref, yc_ref,
                     wg_ref, wu_ref, wd_ref, win_ref,
                     h_ref, acc_ref, ubuf_ref, q_s, k_s, b_s, v_s, g_s, st_ref,
                     *, tiles_per_seq, n_tiles):
    i = pl.program_id(0)
    j = i - CAST_STEPS
    wslot = j % 2
    rslot = 1 - wslot
    staging = (q_s, k_s, b_s, v_s, g_s)

    @pl.when(i < CAST_STEPS)
    def _():
        _cast_weight_rows(i, ((wg_f32, wg_ref), (wu_f32, wu_ref), (wd_f32, wd_ref), (win_f32, win_ref)))

    @pl.when(j == 0)
    def _():
        for ref in staging:
            ref[rslot] = jnp.zeros(ref.shape[1:], F32)

    @pl.when((j + tiles_per_seq - 1) % tiles_per_seq == 0)
    def _():
        st_ref[...] = jnp.zeros_like(st_ref)

    @pl.when(j % tiles_per_seq == 0)
    def _():
        ubuf_ref[0:SUBLANES, :] = jnp.zeros((SUBLANES, CONV_W), F32)

    @pl.when((j >= 0) & (j < n_tiles))
    def _():
        hgrn = _hgrn_pieces(rslot, *staging, gh_ref, yh_ref, st_ref)
        _inproj_tile(wslot, hgrn, (TILE_A // CHUNK) * HGRN_HEADS * HGRN_UNIT_STEPS, x_ref, g1_ref, wg_ref, wu_ref, wd_ref, gm_ref, win_ref, lbp_ref,
                     cw_ref, tri_ref, x1_ref, yc_ref, h_ref, acc_ref, ubuf_ref, *staging)

    @pl.when(j == n_tiles)
    def _():
        for _ in _hgrn_pieces(rslot, *staging, gh_ref, yh_ref, st_ref):
            pass


def _const_spec(shape):
    nd = len(shape)
    return pl.BlockSpec(shape, lambda *_: (0,) * nd, pipeline_mode=pl.Buffered(1))


def _stage_ab(x2d, g1, wg, wu, wd, gm, win, lbp, cw, tri, gh, *, l, seq):
    t = x2d.shape[0]
    tm = TILE_A
    n_tiles = t // tm
    tile = lambda i: jnp.clip(i - CAST_STEPS, 0, n_tiles - 1)
    cur = lambda w: pl.BlockSpec((tm, w), lambda i: (tile(i), 0))
    prev = lambda w: pl.BlockSpec((tm, w), lambda i: (tile(i - 1), 0))
    wspec = lambda w: _weight_block_spec(w, l)
    cspec = lambda c: _const_spec(c.shape)
    stage = pltpu.VMEM((2, tm, PIECE_W), F32)
    return pl.pallas_call(
        functools.partial(_stage_ab_kernel, tiles_per_seq=seq // tm, n_tiles=n_tiles),
        grid=(CAST_STEPS + n_tiles + 1,),
        in_specs=[cur(D_MODEL), cspec(g1), wspec(wg), wspec(wu), wspec(wd), cspec(gm), wspec(win),
                  cspec(lbp), cspec(cw), cspec(tri), cspec(gh)],
        out_specs=[cur(D_MODEL), prev(HGRN_W), cur(CONV_W)],
        out_shape=[jax.ShapeDtypeStruct((t, D_MODEL), F32), jax.ShapeDtypeStruct((t, HGRN_W), BF16),
                   jax.ShapeDtypeStruct((t, CONV_W), BF16)],
        scratch_shapes=[_weight_scratch(wg), _weight_scratch(wu), _weight_scratch(wd), _weight_scratch(win),
                        pltpu.VMEM((tm, D_MODEL), BF16), pltpu.VMEM((tm, D_MODEL), F32),
                        pltpu.VMEM((tm + SUBLANES, CONV_W), F32),
                        stage, stage, stage, stage, stage,
                        pltpu.VMEM((HGRN_HEADS, HGRN_DV, HGRN_DK), F32)],
        compiler_params=pltpu.CompilerParams(dimension_semantics=("arbitrary",),
                                             vmem_limit_bytes=VMEM_LIMIT_BYTES),
        name="stage_ab_ffn1_inproj_hgrn2",
    )(x2d, g1, wg, wu, wd, gm, win, lbp, cw, tri, gh)


def _stage_c_kernel(x1_ref, yh_ref, yc_ref, mem_ref, gmem_ref, wout_f32, gx_ref, wq_f32, wo_f32, wkv_f32,
                    g2_ref, wg_f32, wu_f32, wd_f32, gf_ref,
                    out_ref,
                    wout_ref, wq_ref, wo_ref, wkv_ref, wg_ref, wu_ref, wd_ref,
                    km_ref, vm_ref, h_ref, acc_ref, att_ref, *, tiles_per_seq):
    i = pl.program_id(0)
    j = i - CAST_STEPS

    @pl.when(i < CAST_STEPS)
    def _():
        _cast_weight_rows(i, ((wout_f32, wout_ref), (wq_f32, wq_ref), (wo_f32, wo_ref), (wkv_f32, wkv_ref),
                              (wg_f32, wg_ref), (wu_f32, wu_ref), (wd_f32, wd_ref)))

    @pl.when((j >= 0) & (j % tiles_per_seq == 0))
    def _():
        mn = _rms(mem_ref[...], gmem_ref[...]).astype(BF16)
        kv = _dot(mn, wkv_ref[...])
        km_ref[...] = kv[:, :D_MODEL].astype(BF16)
        vm_ref[...] = kv[:, D_MODEL:].astype(BF16)

    @pl.when(j >= 0)
    def _():
        x2 = (x1_ref[...]
              + _dot(yh_ref[...], wout_ref[0:HGRN_W, :])
              + _dot(yc_ref[...], wout_ref[HGRN_W:HGRN_W + CONV_W, :]))
        hq = _rms(x2, gx_ref[...]).astype(BF16)
        qm = _dot(hq, wq_ref[...]) * (MEM_HD ** -0.5)
        for h in range(MEM_HEADS):
            hs = slice(h * MEM_HD, (h + 1) * MEM_HD)
            sc = _dot_nt(qm[:, hs].astype(BF16), km_ref[:, hs])
            e = jnp.exp(sc - jnp.max(sc, axis=-1, keepdims=True))
            pv = _dot(e.astype(BF16), vm_ref[:, hs])
            att_ref[:, hs] = (pv / jnp.sum(e, axis=-1, keepdims=True)).astype(BF16)
        x3 = x2 + _dot(att_ref[...], wo_ref[...])
        h_ref[...] = _rms(x3, g2_ref[...]).astype(BF16)
        _swiglu_into(h_ref, wg_ref, wu_ref, wd_ref, acc_ref)
        x4 = x3 + 0.5 * acc_ref[...]
        out_ref[...] = _rms(x4, gf_ref[...])


def _stage_c(x1, yh, yc, mem2d, gmem, wout, gx, wq, wo, wkv, g2, wg, wu, wd, gf, *, l, seq):
    t = x1.shape[0]
    tm = TILE_C
    tiles_per_seq = seq // tm
    tile = lambda i: jnp.maximum(i - CAST_STEPS, 0)
    row = lambda w: pl.BlockSpec((tm, w), lambda i: (tile(i), 0))
    memblk = pl.BlockSpec((MEM_LEN, D_MODEL), lambda i: (tile(i) // tiles_per_seq, 0))
    wspec = lambda w: _weight_block_spec(w, l)
    cspec = lambda c: _const_spec(c.shape)
    weights = (wout, wq, wo, wkv, wg, wu, wd)
    return pl.pallas_call(
        functools.partial(_stage_c_kernel, tiles_per_seq=tiles_per_seq),
        grid=(CAST_STEPS + t // tm,),
        in_specs=[row(D_MODEL), row(HGRN_W), row(CONV_W), memblk, cspec(gmem), wspec(wout), cspec(gx),
                  wspec(wq), wspec(wo), wspec(wkv), cspec(g2), wspec(wg), wspec(wu), wspec(wd), cspec(gf)],
        out_specs=row(D_MODEL),
        out_shape=jax.ShapeDtypeStruct((t, D_MODEL), F32),
        scratch_shapes=[_weight_scratch(w) for w in weights] + [
            pltpu.VMEM((MEM_LEN, D_MODEL), BF16), pltpu.VMEM((MEM_LEN, D_MODEL), BF16),
            pltpu.VMEM((tm, D_MODEL), BF16), pltpu.VMEM((tm, D_MODEL), F32),
            pltpu.VMEM((tm, D_MODEL), BF16)],
        compiler_params=pltpu.CompilerParams(dimension_semantics=("arbitrary",),
                                             vmem_limit_bytes=VMEM_LIMIT_BYTES),
        name="stage_c_outproj_xattn_ffn2",
    )(x1, yh, yc, mem2d, gmem, wout, gx, wq, wo, wkv, g2, wg, wu, wd, gf)


def _chunk_tri3():
    r = lax.broadcasted_iota(jnp.int32, (CHUNK, 3 * CHUNK), 0)
    c = lax.broadcasted_iota(jnp.int32, (CHUNK, 3 * CHUNK), 1)
    return (c % CHUNK <= r).astype(BF16)


def kernel(x, mem, ffn1_norm, ffn1_gate, ffn1_up, ffn1_down, mix_norm, w_in, lb_param, hgrn_out_norm, conv_w, w_out, xattn_norm, mem_norm, w_q_mem, w_kv_mem, w_o_mem, ffn2_norm, ffn2_gate, ffn2_up, ffn2_down, final_norm):
    batch, seq, _ = x.shape
    depth = ffn1_norm.shape[0]
    assert depth == 1 and seq % TILE_A == 0 and seq % TILE_C == 0
    l = 0
    x2d = x.reshape(batch * seq, D_MODEL)
    vec = lambda a: a.reshape(1, -1)

    x1, yh, yc = _stage_ab(
        x2d, vec(ffn1_norm[l]), ffn1_gate, ffn1_up, ffn1_down, vec(mix_norm[l]), w_in,
        lb_param, conv_w[l].T, _chunk_tri3(), vec(hgrn_out_norm[l]), l=l, seq=seq)

    out = _stage_c(x1, yh, yc, mem.reshape(batch * MEM_LEN, D_MODEL), vec(mem_norm[l]), w_out,
                   vec(xattn_norm[l]), w_q_mem, w_o_mem, w_kv_mem, vec(ffn2_norm[l]),
                   ffn2_gate, ffn2_up, ffn2_down, vec(final_norm), l=l, seq=seq)
    return out.reshape(batch, seq, D_MODEL)
```

```python
import functools
import math

import jax
import jax.numpy as jnp
from jax import lax
from jax.experimental import pallas as pl
from jax.experimental.pallas import tpu as pltpu

F32 = jnp.float32
BF16 = jnp.bfloat16

D_MODEL = 1024
HGRN_W = 512
CONV_W = 512
HGRN_HEADS = 4
HGRN_DK = 128
HGRN_DV = 128
HGRN_F = HGRN_HEADS * HGRN_DK
CONV_K = 3
CHUNK = 64
MEM_LEN = 256
MEM_HEADS = 4
MEM_HD = D_MODEL // MEM_HEADS
D_FF = int(math.ceil(8 * D_MODEL / 3 / 256) * 256)
EPS = 1e-6
LOG2_E = 1.4426950408889634
N_IN_PIECES = 7
PIECE_W = 512

FF_CHUNK = 256
N_FF_CHUNKS = D_FF // FF_CHUNK
SUBLANES = 8
DIAG_BLOCK = SUBLANES
LEVEL_BLOCKS = (8, 16, 32)

CAST_STEPS = 16
TILE_A = 512
TILE_C = 512
VMEM_LIMIT_BYTES = 60 * 1024 * 1024


def _dot(a, b):
    return jnp.dot(a, b, preferred_element_type=F32)


def _dot_nt(a, b):
    return lax.dot_general(a, b, (((1,), (1,)), ((), ())), preferred_element_type=F32)


def _rms(x, g):
    ms = jnp.mean(x * x, axis=-1, keepdims=True)
    return x * lax.rsqrt(ms + EPS) * g


def _silu(x):
    return x * jax.nn.sigmoid(x)


def _split3(x):
    hi = x.astype(BF16)
    r1 = x - hi.astype(F32)
    mid = r1.astype(BF16)
    r2 = r1 - mid.astype(F32)
    lo = r2.astype(BF16)
    return hi, mid, lo


SIDE_GROUP = 12
K_TILE = 256


def _swiglu_into(h_ref, wg_ref, wu_ref, wd_ref, acc_ref, side_work=None, n_side_steps=0):
    n_slots = N_FF_CHUNKS * 3 * (D_MODEL // K_TILE)
    slot_counter = [0, 0]

    def pull():
        slot_counter[0] += 1
        if slot_counter[0] % SIDE_GROUP:
            return
        due = -(-n_side_steps * slot_counter[0] // n_slots)
        while slot_counter[1] < due:
            next(side_work, None)
            slot_counter[1] += 1

    def dot_ktiles(w_ref, cols):
        out = None
        for kt in range(D_MODEL // K_TILE):
            ks = slice(kt * K_TILE, (kt + 1) * K_TILE)
            part = _dot(h_ref[:, ks], w_ref[ks, cols])
            out = part if out is None else out + part
            pull()
        return out

    for c in range(N_FF_CHUNKS):
        cols = slice(c * FF_CHUNK, (c + 1) * FF_CHUNK)
        g = dot_ktiles(wg_ref, cols)
        u = dot_ktiles(wu_ref, cols)
        a = (_silu(g) * u).astype(BF16)
        for nt in range(D_MODEL // K_TILE):
            ns = slice(nt * K_TILE, (nt + 1) * K_TILE)
            d = _dot(a, wd_ref[cols, ns])
            if c == 0:
                acc_ref[:, ns] = d
            else:
                acc_ref[:, ns] += d
            pull()


def _chunk_masks():
    row = lax.broadcasted_iota(jnp.int32, (CHUNK, CHUNK), 0)
    col = lax.broadcasted_iota(jnp.int32, (CHUNK, CHUNK), 1)
    diag_mask = (row // DIAG_BLOCK == col // DIAG_BLOCK) & (col <= row)
    rk = lax.broadcasted_iota(jnp.int32, (CHUNK, HGRN_DK), 0)
    level_masks, odd_rows = [], []
    for bs in LEVEL_BLOCKS:
        level_masks.append((row // (2 * bs) == col // (2 * bs))
                           & ((row // bs) % 2 == 1) & ((col // bs) % 2 == 0))
        odd_rows.append((rk // bs) % 2 == 1)
    nblk = CHUNK // DIAG_BLOCK
    lane = lax.broadcasted_iota(jnp.int32, (nblk, DIAG_BLOCK, CHUNK), 2)
    blk = lax.broadcasted_iota(jnp.int32, (nblk, DIAG_BLOCK, CHUNK), 0)
    lane_blk = lane - blk * DIAG_BLOCK
    return diag_mask, level_masks, odd_rows, lane_blk


HGRN_UNIT_STEPS = 6 + len(LEVEL_BLOCKS) + DIAG_BLOCK

def _hgrn_unit(slot, c, h, masks, q_s, k_s, b_s, v_s, g_s, gh_ref, y_ref, st_ref):
    diag_mask, level_masks, odd_rows, lane_blk = masks
    rows = slice(c * CHUNK, (c + 1) * CHUNK)
    lanes = slice(h * HGRN_DK, (h + 1) * HGRN_DK)
    vl = slice(h * HGRN_DV, (h + 1) * HGRN_DV)
    q = q_s[slot, rows, lanes]
    kbb = k_s[slot, rows, lanes]
    b2 = b_s[slot, rows, lanes]
    v = v_s[slot, rows, vl]
    vb = v.astype(BF16)
    vbt = v.T.astype(BF16)
    b_last = b2[CHUNK - 1:CHUNK, :]
    qb = (q * jnp.exp2(b2)).astype(BF16)
    kd = jnp.exp2(b_last - kbb).astype(BF16)
    yield
    st = st_ref[h]
    o_inter = _dot_nt(qb, st.astype(BF16))
    upd = _dot(vbt, kd)
    yield

    def level_operands(bs, odd):
        npair = CHUNK // (2 * bs)
        r = b2.reshape(npair, 2 * bs, HGRN_DK)[:, bs:bs + 1, :]
        r = jnp.broadcast_to(r, (npair, 2 * bs, HGRN_DK)).reshape(CHUNK, HGRN_DK)
        e = jnp.exp2(jnp.where(odd, b2 - r, r - kbb))
        return (q * e).astype(BF16), e.astype(BF16)

    operands = level_operands(LEVEL_BLOCKS[0], odd_rows[0])
    st_ref[h] = st * jnp.exp2(b_last) + upd
    yield
    level_scores = []
    for nxt in range(1, len(LEVEL_BLOCKS) + 1):
        level_scores.append(_dot_nt(*operands))
        if nxt < len(LEVEL_BLOCKS):
            operands = level_operands(LEVEL_BLOCKS[nxt], odd_rows[nxt])
        yield
    nblk = CHUNK // DIAG_BLOCK
    q3 = q.reshape(nblk, DIAG_BLOCK, HGRN_DK)
    b3 = b2.reshape(nblk, DIAG_BLOCK, HGRN_DK)
    kbb3 = kbb.reshape(nblk, DIAG_BLOCK, HGRN_DK)
    s3 = jnp.zeros((nblk, DIAG_BLOCK, CHUNK), F32)
    for s in range(DIAG_BLOCK):
        tmp = q3 * jnp.exp2(b3 - kbb3[:, s:s + 1, :])
        red = jnp.sum(tmp, axis=-1, keepdims=True)
        s3 = jnp.where(lane_blk == s, red, s3)
        yield
    scores = jnp.where(diag_mask, s3.reshape(CHUNK, CHUNK), 0.0)
    for lmask, ls in zip(level_masks, level_scores):
        scores = jnp.where(lmask, ls, scores)
    scores = scores.astype(BF16)
    yield
    o_intra = _dot(scores, vb)
    yield
    o = o_inter + o_intra
    o = o * lax.rsqrt(jnp.mean(o * o, axis=-1, keepdims=True) + EPS) * gh_ref[:, vl]
    y_ref[rows, vl] = (o * g_s[slot, rows, vl]).astype(y_ref.dtype)
    yield


def _hgrn_pieces(slot, q_s, k_s, b_s, v_s, g_s, gh_ref, y_ref, st_ref):
    masks = _chunk_masks()
    for c in range(q_s.shape[1] // CHUNK):
        units = [_hgrn_unit(slot, c, h, masks, q_s, k_s, b_s, v_s, g_s, gh_ref, y_ref, st_ref)
                 for h in range(HGRN_HEADS)]
        while units:
            for u in list(units):
                try:
                    next(u)
                    yield
                except StopIteration:
                    units.remove(u)


def _cast_weight_rows(i, pairs):
    for w_ref, dst_ref in pairs:
        rb = w_ref.shape[0]
        dst_ref[pl.ds(pl.multiple_of(i * rb, rb), rb), :] = w_ref[...].astype(dst_ref.dtype)


def _weight_block_spec(w, l):
    _, rows, cols = w.shape
    return pl.BlockSpec((None, rows // CAST_STEPS, cols),
                        lambda i: (l, jnp.minimum(i, CAST_STEPS - 1), 0))


def _weight_scratch(w):
    return pltpu.VMEM(w.shape[1:], BF16)


def _inproj_tile(slot, side_work, n_side_steps, x_ref, g1_ref, wg_ref, wu_ref, wd_ref, gm_ref, win_ref, lbp_ref,
                 cw_ref, tri_ref, x1_ref, yc_ref, h_ref, acc_ref, ubuf_ref, q_s, k_s, b_s, v_s, g_s):
    tm = x_ref.shape[0]
    x = x_ref[...]
    h_ref[...] = _rms(x, g1_ref[...]).astype(BF16)
    _swiglu_into(h_ref, wg_ref, wu_ref, wd_ref, acc_ref, side_work, n_side_steps)
    x1 = x + 0.5 * acc_ref[...]
    x1_ref[...] = x1
    h_ref[...] = _rms(x1, gm_ref[...]).astype(BF16)

    def proj(p):
        return _dot(h_ref[...], win_ref[:, p * PIECE_W:(p + 1) * PIECE_W])

    lbp = lbp_ref[...]
    lbe = jnp.exp(lbp - jnp.max(lbp, axis=0, keepdims=True))
    lb = lbe[0:1, :] / jnp.sum(lbe, axis=0, keepdims=True)

    u = proj(5) * proj(6)
    ubuf_ref[SUBLANES:SUBLANES + tm, :] = u
    u1 = ubuf_ref[pl.ds(SUBLANES - 1, tm), :]
    u2 = ubuf_ref[pl.ds(SUBLANES - 2, tm), :]
    cw = cw_ref[...]
    yc_ref[...] = (proj(4) * (cw[0:1, :] * u2 + cw[1:2, :] * u1 + cw[2:3, :] * u)).astype(yc_ref.dtype)
    ubuf_ref[0:SUBLANES, :] = ubuf_ref[tm:tm + SUBLANES, :]

    f = lb + (1.0 - lb) * jax.nn.sigmoid(proj(1))
    log2f = jnp.log(f) * LOG2_E
    hi, mid, lo = _split3(log2f)
    tri3 = tri_ref[...]
    b2 = jnp.concatenate(
        [_dot(tri3, jnp.concatenate([t[c * CHUNK:(c + 1) * CHUNK] for t in (hi, mid, lo)], axis=0))
         for c in range(tm // CHUNK)], axis=0)
    b_s[slot] = b2
    k_s[slot] = b2 - jnp.log(1.0 - f) * LOG2_E

    q_s[slot] = _silu(proj(0)) * (HGRN_DK ** -0.5)
    g_s[slot] = _silu(proj(3))
    v_s[slot] = proj(2)
    for _ in side_work:
        pass


def _stage_ab_kernel(x_ref, g1_ref, wg_f32, wu_f32, wd_f32, gm_ref, win_f32, lbp_ref, cw_ref,
                     tri_ref, gh_ref,
                     x1_ref, yh_ref, yc_ref,
                     wg_ref, wu_ref, wd_ref, win_ref,
                     h_ref, acc_ref, ubuf_ref, q_s, k_s, b_s, v_s, g_s, st_ref,
                     *, tiles_per_seq, n_tiles):
    i = pl.program_id(0)
    j = i - CAST_STEPS
    wslot = j % 2
    rslot = 1 - wslot
    staging = (q_s, k_s, b_s, v_s, g_s)

    @pl.when(i < CAST_STEPS)
    def _():
        _cast_weight_rows(i, ((wg_f32, wg_ref), (wu_f32, wu_ref), (wd_f32, wd_ref), (win_f32, win_ref)))

    @pl.when(j == 0)
    def _():
        for ref in staging:
            ref[rslot] = jnp.zeros(ref.shape[1:], F32)

    @pl.when((j + tiles_per_seq - 1) % tiles_per_seq == 0)
    def _():
        st_ref[...] = jnp.zeros_like(st_ref)

    @pl.when(j % tiles_per_seq == 0)
    def _():
        ubuf_ref[0:SUBLANES, :] = jnp.zeros((SUBLANES, CONV_W), F32)

    @pl.when((j >= 0) & (j < n_tiles))
    def _():
        hgrn = _hgrn_pieces(rslot, *staging, gh_ref, yh_ref, st_ref)
        _inproj_tile(wslot, hgrn, (TILE_A // CHUNK) * HGRN_HEADS * HGRN_UNIT_STEPS, x_ref, g1_ref, wg_ref, wu_ref, wd_ref, gm_ref, win_ref, lbp_ref,
                     cw_ref, tri_ref, x1_ref, yc_ref, h_ref, acc_ref, ubuf_ref, *staging)

    @pl.when(j == n_tiles)
    def _():
        for _ in _hgrn_pieces(rslot, *staging, gh_ref, yh_ref, st_ref):
            pass


def _const_spec(shape):
    nd = len(shape)
    return pl.BlockSpec(shape, lambda *_: (0,) * nd, pipeline_mode=pl.Buffered(1))


def _stage_ab(x2d, g1, wg, wu, wd, gm, win, lbp, cw, tri, gh, *, l, seq):
    t = x2d.shape[0]
    tm = TILE_A
    n_tiles = t // tm
    tile = lambda i: jnp.clip(i - CAST_STEPS, 0, n_tiles - 1)
    cur = lambda w: pl.BlockSpec((tm, w), lambda i: (tile(i), 0))
    prev = lambda w: pl.BlockSpec((tm, w), lambda i: (tile(i - 1), 0))
    wspec = lambda w: _weight_block_spec(w, l)
    cspec = lambda c: _const_spec(c.shape)
    stage = pltpu.VMEM((2, tm, PIECE_W), F32)
    return pl.pallas_call(
        functools.partial(_stage_ab_kernel, tiles_per_seq=seq // tm, n_tiles=n_tiles),
        grid=(CAST_STEPS + n_tiles + 1,),
        in_specs=[cur(D_MODEL), cspec(g1), wspec(wg), wspec(wu), wspec(wd), cspec(gm), wspec(win),
                  cspec(lbp), cspec(cw), cspec(tri), cspec(gh)],
        out_specs=[cur(D_MODEL), prev(HGRN_W), cur(CONV_W)],
        out_shape=[jax.ShapeDtypeStruct((t, D_MODEL), F32), jax.ShapeDtypeStruct((t, HGRN_W), BF16),
                   jax.ShapeDtypeStruct((t, CONV_W), BF16)],
        scratch_shapes=[_weight_scratch(wg), _weight_scratch(wu), _weight_scratch(wd), _weight_scratch(win),
                        pltpu.VMEM((tm, D_MODEL), BF16), pltpu.VMEM((tm, D_MODEL), F32),
                        pltpu.VMEM((tm + SUBLANES, CONV_W), F32),
                        stage, stage, stage, stage, stage,
                        pltpu.VMEM((HGRN_HEADS, HGRN_DV, HGRN_DK), F32)],
        compiler_params=pltpu.CompilerParams(dimension_semantics=("arbitrary",),
                                             vmem_limit_bytes=VMEM_LIMIT_BYTES),
        name="stage_ab_ffn1_inproj_hgrn2",
    )(x2d, g1, wg, wu, wd, gm, win, lbp, cw, tri, gh)


def _stage_c_kernel(x1_ref, yh_ref, yc_ref, mem_ref, gmem_ref, wout_f32, gx_ref, wq_f32, wo_f32, wkv_f32,
                    g2_ref, wg_f32, wu_f32, wd_f32, gf_ref,
                    out_ref,
                    wout_ref, wq_ref, wo_ref, wkv_ref, wg_ref, wu_ref, wd_ref,
                    km_ref, vm_ref, h_ref, acc_ref, att_ref, *, tiles_per_seq):
    i = pl.program_id(0)
    j = i - CAST_STEPS

    @pl.when(i < CAST_STEPS)
    def _():
        _cast_weight_rows(i, ((wout_f32, wout_ref), (wq_f32, wq_ref), (wo_f32, wo_ref), (wkv_f32, wkv_ref),
                              (wg_f32, wg_ref), (wu_f32, wu_ref), (wd_f32, wd_ref)))

    @pl.when((j >= 0) & (j % tiles_per_seq == 0))
    def _():
        mn = _rms(mem_ref[...], gmem_ref[...]).astype(BF16)
        kv = _dot(mn, wkv_ref[...])
        km_ref[...] = kv[:, :D_MODEL].astype(BF16)
        vm_ref[...] = kv[:, D_MODEL:].astype(BF16)

    @pl.when(j >= 0)
    def _():
        x2 = (x1_ref[...]
              + _dot(yh_ref[...], wout_ref[0:HGRN_W, :])
              + _dot(yc_ref[...], wout_ref[HGRN_W:HGRN_W + CONV_W, :]))
        hq = _rms(x2, gx_ref[...]).astype(BF16)
        qm = _dot(hq, wq_ref[...]) * (MEM_HD ** -0.5)
        head = lambda h: slice(h * MEM_HD, (h + 1) * MEM_HD)
        scores = lambda h: _dot_nt(qm[:, head(h)].astype(BF16), km_ref[:, head(h)])
        sc = scores(0)
        for h in range(MEM_HEADS):
            sc_next = scores(h + 1) if h + 1 < MEM_HEADS else None
            e = jnp.exp(sc - jnp.max(sc, axis=-1, keepdims=True))
            pv = _dot(e.astype(BF16), vm_ref[:, head(h)])
            att_ref[:, head(h)] = (pv / jnp.sum(e, axis=-1, keepdims=True)).astype(BF16)
            sc = sc_next
        x3 = x2 + _dot(att_ref[...], wo_ref[...])
        h_ref[...] = _rms(x3, g2_ref[...]).astype(BF16)
        _swiglu_into(h_ref, wg_ref, wu_ref, wd_ref, acc_ref)
        x4 = x3 + 0.5 * acc_ref[...]
        out_ref[...] = _rms(x4, gf_ref[...])


def _stage_c(x1, yh, yc, mem2d, gmem, wout, gx, wq, wo, wkv, g2, wg, wu, wd, gf, *, l, seq):
    t = x1.shape[0]
    tm = TILE_C
    tiles_per_seq = seq // tm
    tile = lambda i: jnp.maximum(i - CAST_STEPS, 0)
    row = lambda w: pl.BlockSpec((tm, w), lambda i: (tile(i), 0))
    memblk = pl.BlockSpec((MEM_LEN, D_MODEL), lambda i: (tile(i) // tiles_per_seq, 0))
    wspec = lambda w: _weight_block_spec(w, l)
    cspec = lambda c: _const_spec(c.shape)
    weights = (wout, wq, wo, wkv, wg, wu, wd)
    return pl.pallas_call(
        functools.partial(_stage_c_kernel, tiles_per_seq=tiles_per_seq),
        grid=(CAST_STEPS + t // tm,),
        in_specs=[row(D_MODEL), row(HGRN_W), row(CONV_W), memblk, cspec(gmem), wspec(wout), cspec(gx),
                  wspec(wq), wspec(wo), wspec(wkv), cspec(g2), wspec(wg), wspec(wu), wspec(wd), cspec(gf)],
        out_specs=row(D_MODEL),
        out_shape=jax.ShapeDtypeStruct((t, D_MODEL), F32),
        scratch_shapes=[_weight_scratch(w) for w in weights] + [
            pltpu.VMEM((MEM_LEN, D_MODEL), BF16), pltpu.VMEM((MEM_LEN, D_MODEL), BF16),
            pltpu.VMEM((tm, D_MODEL), BF16), pltpu.VMEM((tm, D_MODEL), F32),
            pltpu.VMEM((tm, D_MODEL), BF16)],
        compiler_params=pltpu.CompilerParams(dimension_semantics=("arbitrary",),
                                             vmem_limit_bytes=VMEM_LIMIT_BYTES),
        name="stage_c_outproj_xattn_ffn2",
    )(x1, yh, yc, mem2d, gmem, wout, gx, wq, wo, wkv, g2, wg, wu, wd, gf)


def _chunk_tri3():
    r = lax.broadcasted_iota(jnp.int32, (CHUNK, 3 * CHUNK), 0)
    c = lax.broadcasted_iota(jnp.int32, (CHUNK, 3 * CHUNK), 1)
    return (c % CHUNK <= r).astype(BF16)


def kernel(x, mem, ffn1_norm, ffn1_gate, ffn1_up, ffn1_down, mix_norm, w_in, lb_param, hgrn_out_norm, conv_w, w_out, xattn_norm, mem_norm, w_q_mem, w_kv_mem, w_o_mem, ffn2_norm, ffn2_gate, ffn2_up, ffn2_down, final_norm):
    batch, seq, _ = x.shape
    depth = ffn1_norm.shape[0]
    assert depth == 1 and seq % TILE_A == 0 and seq % TILE_C == 0
    l = 0
    x2d = x.reshape(batch * seq, D_MODEL)
    vec = lambda a: a.reshape(1, -1)

    x1, yh, yc = _stage_ab(
        x2d, vec(ffn1_norm[l]), ffn1_gate, ffn1_up, ffn1_down, vec(mix_norm[l]), w_in,
        lb_param, conv_w[l].T, _chunk_tri3(), vec(hgrn_out_norm[l]), l=l, seq=seq)

    out = _stage_c(x1, yh, yc, mem.reshape(batch * MEM_LEN, D_MODEL), vec(mem_norm[l]), w_out,
                   vec(xattn_norm[l]), w_q_mem, w_o_mem, w_kv_mem, vec(ffn2_norm[l]),
                   ffn2_gate, ffn2_up, ffn2_down, vec(final_norm), l=l, seq=seq)
    return out.reshape(batch, seq, D_MODEL)
```

```python
import functools
import math

import jax
import jax.numpy as jnp
from jax import lax
from jax.experimental import pallas as pl
from jax.experimental.pallas import tpu as pltpu

F32 = jnp.float32
BF16 = jnp.bfloat16

D_MODEL = 1024
HGRN_W = 512
CONV_W = 512
HGRN_HEADS = 4
HGRN_DK = 128
HGRN_DV = 128
HGRN_F = HGRN_HEADS * HGRN_DK
CONV_K = 3
CHUNK = 64
MEM_LEN = 256
MEM_HEADS = 4
MEM_HD = D_MODEL // MEM_HEADS
D_FF = int(math.ceil(8 * D_MODEL / 3 / 256) * 256)
EPS = 1e-6
LOG2_E = 1.4426950408889634
N_IN_PIECES = 7
PIECE_W = 512

FF_CHUNK = 256
N_FF_CHUNKS = D_FF // FF_CHUNK
SUBLANES = 8
DIAG_BLOCK = SUBLANES
LEVEL_BLOCKS = (8, 16, 32)

CAST_STEPS = 16
CAST_STEPS_C = 8
TILE_A = 512
TILE_C = 512
VMEM_LIMIT_BYTES = 60 * 1024 * 1024


def _dot(a, b):
    return jnp.dot(a, b, preferred_element_type=F32)


def _dot_nt(a, b):
    return lax.dot_general(a, b, (((1,), (1,)), ((), ())), preferred_element_type=F32)


def _rms(x, g):
    ms = jnp.mean(x * x, axis=-1, keepdims=True)
    return x * lax.rsqrt(ms + EPS) * g


def _silu(x):
    return x * jax.nn.sigmoid(x)


def _split3(x):
    hi = x.astype(BF16)
    r1 = x - hi.astype(F32)
    mid = r1.astype(BF16)
    r2 = r1 - mid.astype(F32)
    lo = r2.astype(BF16)
    return hi, mid, lo


SIDE_GROUP = 12
K_TILE = 256


def _swiglu_into(h_ref, wg_ref, wu_ref, wd_ref, acc_ref, side_work=None, n_side_steps=0):
    n_slots = N_FF_CHUNKS * 3 * (D_MODEL // K_TILE)
    slot_counter = [0, 0]

    def pull():
        slot_counter[0] += 1
        if slot_counter[0] % SIDE_GROUP:
            return
        due = -(-n_side_steps * slot_counter[0] // n_slots)
        while slot_counter[1] < due:
            next(side_work, None)
            slot_counter[1] += 1

    def dot_ktiles(w_ref, cols):
        out = None
        for kt in range(D_MODEL // K_TILE):
            ks = slice(kt * K_TILE, (kt + 1) * K_TILE)
            part = _dot(h_ref[:, ks], w_ref[ks, cols])
            out = part if out is None else out + part
            pull()
        return out

    for c in range(N_FF_CHUNKS):
        cols = slice(c * FF_CHUNK, (c + 1) * FF_CHUNK)
        g = dot_ktiles(wg_ref, cols)
        u = dot_ktiles(wu_ref, cols)
        a = (_silu(g) * u).astype(BF16)
        for nt in range(D_MODEL // K_TILE):
            ns = slice(nt * K_TILE, (nt + 1) * K_TILE)
            d = _dot(a, wd_ref[cols, ns])
            if c == 0:
                acc_ref[:, ns] = d
            else:
                acc_ref[:, ns] += d
            pull()


def _chunk_masks():
    row = lax.broadcasted_iota(jnp.int32, (CHUNK, CHUNK), 0)
    col = lax.broadcasted_iota(jnp.int32, (CHUNK, CHUNK), 1)
    diag_mask = (row // DIAG_BLOCK == col // DIAG_BLOCK) & (col <= row)
    rk = lax.broadcasted_iota(jnp.int32, (CHUNK, HGRN_DK), 0)
    level_masks, odd_rows = [], []
    for bs in LEVEL_BLOCKS:
        level_masks.append((row // (2 * bs) == col // (2 * bs))
                           & ((row // bs) % 2 == 1) & ((col // bs) % 2 == 0))
        odd_rows.append((rk // bs) % 2 == 1)
    nblk = CHUNK // DIAG_BLOCK
    lane = lax.broadcasted_iota(jnp.int32, (nblk, DIAG_BLOCK, CHUNK), 2)
    blk = lax.broadcasted_iota(jnp.int32, (nblk, DIAG_BLOCK, CHUNK), 0)
    lane_blk = lane - blk * DIAG_BLOCK
    return diag_mask, level_masks, odd_rows, lane_blk


HGRN_UNIT_STEPS = 6 + len(LEVEL_BLOCKS) + DIAG_BLOCK

def _hgrn_unit(slot, c, h, masks, q_s, k_s, b_s, v_s, g_s, gh_ref, y_ref, st_ref):
    diag_mask, level_masks, odd_rows, lane_blk = masks
    rows = slice(c * CHUNK, (c + 1) * CHUNK)
    lanes = slice(h * HGRN_DK, (h + 1) * HGRN_DK)
    vl = slice(h * HGRN_DV, (h + 1) * HGRN_DV)
    q = q_s[slot, rows, lanes]
    kbb = k_s[slot, rows, lanes]
    b2 = b_s[slot, rows, lanes]
    v = v_s[slot, rows, vl]
    vb = v.astype(BF16)
    vbt = v.T.astype(BF16)
    b_last = b2[CHUNK - 1:CHUNK, :]
    qb = (q * jnp.exp2(b2)).astype(BF16)
    kd = jnp.exp2(b_last - kbb).astype(BF16)
    yield
    st = st_ref[h]
    o_inter = _dot_nt(qb, st.astype(BF16))
    upd = _dot(vbt, kd)
    yield

    def level_operands(bs, odd):
        npair = CHUNK // (2 * bs)
        r = b2.reshape(npair, 2 * bs, HGRN_DK)[:, bs:bs + 1, :]
        r = jnp.broadcast_to(r, (npair, 2 * bs, HGRN_DK)).reshape(CHUNK, HGRN_DK)
        e = jnp.exp2(jnp.where(odd, b2 - r, r - kbb))
        return (q * e).astype(BF16), e.astype(BF16)

    operands = level_operands(LEVEL_BLOCKS[0], odd_rows[0])
    st_ref[h] = st * jnp.exp2(b_last) + upd
    yield
    level_scores = []
    for nxt in range(1, len(LEVEL_BLOCKS) + 1):
        level_scores.append(_dot_nt(*operands))
        if nxt < len(LEVEL_BLOCKS):
            operands = level_operands(LEVEL_BLOCKS[nxt], odd_rows[nxt])
        yield
    nblk = CHUNK // DIAG_BLOCK
    q3 = q.reshape(nblk, DIAG_BLOCK, HGRN_DK)
    b3 = b2.reshape(nblk, DIAG_BLOCK, HGRN_DK)
    kbb3 = kbb.reshape(nblk, DIAG_BLOCK, HGRN_DK)
    s3 = jnp.zeros((nblk, DIAG_BLOCK, CHUNK), F32)
    for s in range(DIAG_BLOCK):
        tmp = q3 * jnp.exp2(b3 - kbb3[:, s:s + 1, :])
        red = jnp.sum(tmp, axis=-1, keepdims=True)
        s3 = jnp.where(lane_blk == s, red, s3)
        yield
    scores = jnp.where(diag_mask, s3.reshape(CHUNK, CHUNK), 0.0)
    for lmask, ls in zip(level_masks, level_scores):
        scores = jnp.where(lmask, ls, scores)
    scores = scores.astype(BF16)
    yield
    o_intra = _dot(scores, vb)
    yield
    o = o_inter + o_intra
    o = o * lax.rsqrt(jnp.mean(o * o, axis=-1, keepdims=True) + EPS) * gh_ref[:, vl]
    y_ref[rows, vl] = (o * g_s[slot, rows, vl]).astype(y_ref.dtype)
    yield


def _hgrn_pieces(slot, q_s, k_s, b_s, v_s, g_s, gh_ref, y_ref, st_ref):
    masks = _chunk_masks()
    for c in range(q_s.shape[1] // CHUNK):
        units = [_hgrn_unit(slot, c, h, masks, q_s, k_s, b_s, v_s, g_s, gh_ref, y_ref, st_ref)
                 for h in range(HGRN_HEADS)]
        while units:
            for u in list(units):
                try:
                    next(u)
                    yield
                except StopIteration:
                    units.remove(u)


def _cast_weight_rows(i, pairs):
    for w_ref, dst_ref in pairs:
        rb = w_ref.shape[0]
        dst_ref[pl.ds(pl.multiple_of(i * rb, rb), rb), :] = w_ref[...].astype(dst_ref.dtype)


def _weight_block_spec(w, l, steps):
    _, rows, cols = w.shape
    return pl.BlockSpec((None, rows // steps, cols), lambda i: (l, jnp.minimum(i, steps - 1), 0))


def _weight_scratch(w):
    return pltpu.VMEM(w.shape[1:], BF16)


def _inproj_tile(slot, side_work, n_side_steps, x_ref, g1_ref, wg_ref, wu_ref, wd_ref, gm_ref, win_ref, lbp_ref,
                 cw_ref, tri_ref, x1_ref, yc_ref, h_ref, acc_ref, ubuf_ref, q_s, k_s, b_s, v_s, g_s):
    tm = x_ref.shape[0]
    x = x_ref[...]
    h_ref[...] = _rms(x, g1_ref[...]).astype(BF16)
    _swiglu_into(h_ref, wg_ref, wu_ref, wd_ref, acc_ref, side_work, n_side_steps)
    x1 = x + 0.5 * acc_ref[...]
    x1_ref[...] = x1
    h_ref[...] = _rms(x1, gm_ref[...]).astype(BF16)

    def proj(p):
        return _dot(h_ref[...], win_ref[:, p * PIECE_W:(p + 1) * PIECE_W])

    lbp = lbp_ref[...]
    lbe = jnp.exp(lbp - jnp.max(lbp, axis=0, keepdims=True))
    lb = lbe[0:1, :] / jnp.sum(lbe, axis=0, keepdims=True)

    u = proj(5) * proj(6)
    ubuf_ref[SUBLANES:SUBLANES + tm, :] = u
    u1 = ubuf_ref[pl.ds(SUBLANES - 1, tm), :]
    u2 = ubuf_ref[pl.ds(SUBLANES - 2, tm), :]
    cw = cw_ref[...]
    yc_ref[...] = (proj(4) * (cw[0:1, :] * u2 + cw[1:2, :] * u1 + cw[2:3, :] * u)).astype(yc_ref.dtype)
    ubuf_ref[0:SUBLANES, :] = ubuf_ref[tm:tm + SUBLANES, :]

    f = lb + (1.0 - lb) * jax.nn.sigmoid(proj(1))
    log2f = jnp.log(f) * LOG2_E
    hi, mid, lo = _split3(log2f)
    tri3 = tri_ref[...]
    b2 = jnp.concatenate(
        [_dot(tri3, jnp.concatenate([t[c * CHUNK:(c + 1) * CHUNK] for t in (hi, mid, lo)], axis=0))
         for c in range(tm // CHUNK)], axis=0)
    b_s[slot] = b2
    k_s[slot] = b2 - jnp.log(1.0 - f) * LOG2_E

    q_s[slot] = _silu(proj(0)) * (HGRN_DK ** -0.5)
    g_s[slot] = _silu(proj(3))
    v_s[slot] = proj(2)
    for _ in side_work:
        pass


def _stage_ab_kernel(x_ref, g1_ref, wg_f32, wu_f32, wd_f32, gm_ref, win_f32, lbp_ref, cw_ref,
                     tri_ref, gh_ref,
                     x1_ref, yh_ref, yc_ref,
                     wg_ref, wu_ref, wd_ref, win_ref,
                     h_ref, acc_ref, ubuf_ref, q_s, k_s, b_s, v_s, g_s, st_ref,
                     *, tiles_per_seq, n_tiles):
    i = pl.program_id(0)
    j = i - CAST_STEPS
    wslot = j % 2
    rslot = 1 - wslot
    staging = (q_s, k_s, b_s, v_s, g_s)

    @pl.when(i < CAST_STEPS)
    def _():
        _cast_weight_rows(i, ((wg_f32, wg_ref), (wu_f32, wu_ref), (wd_f32, wd_ref), (win_f32, win_ref)))

    @pl.when(j == 0)
    def _():
        for ref in staging:
            ref[rslot] = jnp.zeros(ref.shape[1:], F32)

    @pl.when((j + tiles_per_seq - 1) % tiles_per_seq == 0)
    def _():
        st_ref[...] = jnp.zeros_like(st_ref)

    @pl.when(j % tiles_per_seq == 0)
    def _():
        ubuf_ref[0:SUBLANES, :] = jnp.zeros((SUBLANES, CONV_W), F32)

    @pl.when((j >= 0) & (j < n_tiles))
    def _():
        hgrn = _hgrn_pieces(rslot, *staging, gh_ref, yh_ref, st_ref)
        _inproj_tile(wslot, hgrn, (TILE_A // CHUNK) * HGRN_HEADS * HGRN_UNIT_STEPS, x_ref, g1_ref, wg_ref, wu_ref, wd_ref, gm_ref, win_ref, lbp_ref,
                     cw_ref, tri_ref, x1_ref, yc_ref, h_ref, acc_ref, ubuf_ref, *staging)

    @pl.when(j == n_tiles)
    def _():
        for _ in _hgrn_pieces(rslot, *staging, gh_ref, yh_ref, st_ref):
            pass


def _const_spec(shape):
    nd = len(shape)
    return pl.BlockSpec(shape, lambda *_: (0,) * nd, pipeline_mode=pl.Buffered(1))


def _stage_ab(x2d, g1, wg, wu, wd, gm, win, lbp, cw, tri, gh, *, l, seq):
    t = x2d.shape[0]
    tm = TILE_A
    n_tiles = t // tm
    tile = lambda i: jnp.clip(i - CAST_STEPS, 0, n_tiles - 1)
    cur = lambda w: pl.BlockSpec((tm, w), lambda i: (tile(i), 0))
    prev = lambda w: pl.BlockSpec((tm, w), lambda i: (tile(i - 1), 0))
    wspec = lambda w: _weight_block_spec(w, l, CAST_STEPS)
    cspec = lambda c: _const_spec(c.shape)
    stage = pltpu.VMEM((2, tm, PIECE_W), F32)
    return pl.pallas_call(
        functools.partial(_stage_ab_kernel, tiles_per_seq=seq // tm, n_tiles=n_tiles),
        grid=(CAST_STEPS + n_tiles + 1,),
        in_specs=[cur(D_MODEL), cspec(g1), wspec(wg), wspec(wu), wspec(wd), cspec(gm), wspec(win),
                  cspec(lbp), cspec(cw), cspec(tri), cspec(gh)],
        out_specs=[cur(D_MODEL), prev(HGRN_W), cur(CONV_W)],
        out_shape=[jax.ShapeDtypeStruct((t, D_MODEL), F32), jax.ShapeDtypeStruct((t, HGRN_W), BF16),
                   jax.ShapeDtypeStruct((t, CONV_W), BF16)],
        scratch_shapes=[_weight_scratch(wg), _weight_scratch(wu), _weight_scratch(wd), _weight_scratch(win),
                        pltpu.VMEM((tm, D_MODEL), BF16), pltpu.VMEM((tm, D_MODEL), F32),
                        pltpu.VMEM((tm + SUBLANES, CONV_W), F32),
                        stage, stage, stage, stage, stage,
                        pltpu.VMEM((HGRN_HEADS, HGRN_DV, HGRN_DK), F32)],
        compiler_params=pltpu.CompilerParams(dimension_semantics=("arbitrary",),
                                             vmem_limit_bytes=VMEM_LIMIT_BYTES),
        name="stage_ab_ffn1_inproj_hgrn2",
    )(x2d, g1, wg, wu, wd, gm, win, lbp, cw, tri, gh)


def _stage_c_kernel(x1_ref, yh_ref, yc_ref, mem_ref, gmem_ref, wout_f32, gx_ref, wq_f32, wo_f32, wkv_f32,
                    g2_ref, wg_f32, wu_f32, wd_f32, gf_ref,
                    out_ref,
                    wout_ref, wq_ref, wo_ref, wkv_ref, wg_ref, wu_ref, wd_ref,
                    km_ref, vm_ref, h_ref, acc_ref, att_ref, *, tiles_per_seq):
    i = pl.program_id(0)
    j = i - CAST_STEPS_C

    @pl.when(i < CAST_STEPS_C)
    def _():
        _cast_weight_rows(i, ((wout_f32, wout_ref), (wq_f32, wq_ref), (wo_f32, wo_ref), (wkv_f32, wkv_ref),
                              (wg_f32, wg_ref), (wu_f32, wu_ref), (wd_f32, wd_ref)))

    @pl.when((j >= 0) & (j % tiles_per_seq == 0))
    def _():
        mn = _rms(mem_ref[...], gmem_ref[...]).astype(BF16)
        kv = _dot(mn, wkv_ref[...])
        km_ref[...] = kv[:, :D_MODEL].astype(BF16)
        vm_ref[...] = kv[:, D_MODEL:].astype(BF16)

    @pl.when(j >= 0)
    def _():
        x2 = (x1_ref[...]
              + _dot(yh_ref[...], wout_ref[0:HGRN_W, :])
              + _dot(yc_ref[...], wout_ref[HGRN_W:HGRN_W + CONV_W, :]))
        hq = _rms(x2, gx_ref[...]).astype(BF16)
        qm = _dot(hq, wq_ref[...]) * (MEM_HD ** -0.5)
        head = lambda h: slice(h * MEM_HD, (h + 1) * MEM_HD)
        scores = lambda h: _dot_nt(qm[:, head(h)].astype(BF16), km_ref[:, head(h)])
        sc = scores(0)
        for h in range(MEM_HEADS):
            sc_next = scores(h + 1) if h + 1 < MEM_HEADS else None
            e = jnp.exp(sc - jnp.max(sc, axis=-1, keepdims=True))
            pv = _dot(e.astype(BF16), vm_ref[:, head(h)])
            att_ref[:, head(h)] = (pv / jnp.sum(e, axis=-1, keepdims=True)).astype(BF16)
            sc = sc_next
        x3 = x2 + _dot(att_ref[...], wo_ref[...])
        h_ref[...] = _rms(x3, g2_ref[...]).astype(BF16)
        _swiglu_into(h_ref, wg_ref, wu_ref, wd_ref, acc_ref)
        x4 = x3 + 0.5 * acc_ref[...]
        out_ref[...] = _rms(x4, gf_ref[...])


def _stage_c(x1, yh, yc, mem2d, gmem, wout, gx, wq, wo, wkv, g2, wg, wu, wd, gf, *, l, seq):
    t = x1.shape[0]
    tm = TILE_C
    tiles_per_seq = seq // tm
    tile = lambda i: jnp.maximum(i - CAST_STEPS_C, 0)
    row = lambda w: pl.BlockSpec((tm, w), lambda i: (tile(i), 0))
    memblk = pl.BlockSpec((MEM_LEN, D_MODEL), lambda i: (tile(i) // tiles_per_seq, 0))
    wspec = lambda w: _weight_block_spec(w, l, CAST_STEPS_C)
    cspec = lambda c: _const_spec(c.shape)
    weights = (wout, wq, wo, wkv, wg, wu, wd)
    return pl.pallas_call(
        functools.partial(_stage_c_kernel, tiles_per_seq=tiles_per_seq),
        grid=(CAST_STEPS_C + t // tm,),
        in_specs=[row(D_MODEL), row(HGRN_W), row(CONV_W), memblk, cspec(gmem), wspec(wout), cspec(gx),
                  wspec(wq), wspec(wo), wspec(wkv), cspec(g2), wspec(wg), wspec(wu), wspec(wd), cspec(gf)],
        out_specs=row(D_MODEL),
        out_shape=jax.ShapeDtypeStruct((t, D_MODEL), F32),
        scratch_shapes=[_weight_scratch(w) for w in weights] + [
            pltpu.VMEM((MEM_LEN, D_MODEL), BF16), pltpu.VMEM((MEM_LEN, D_MODEL), BF16),
            pltpu.VMEM((tm, D_MODEL), BF16), pltpu.VMEM((tm, D_MODEL), F32),
            pltpu.VMEM((tm, D_MODEL), BF16)],
        compiler_params=pltpu.CompilerParams(dimension_semantics=("arbitrary",),
                                             vmem_limit_bytes=VMEM_LIMIT_BYTES),
        name="stage_c_outproj_xattn_ffn2",
    )(x1, yh, yc, mem2d, gmem, wout, gx, wq, wo, wkv, g2, wg, wu, wd, gf)


def _chunk_tri3():
    r = lax.broadcasted_iota(jnp.int32, (CHUNK, 3 * CHUNK), 0)
    c = lax.broadcasted_iota(jnp.int32, (CHUNK, 3 * CHUNK), 1)
    return (c % CHUNK <= r).astype(BF16)


def kernel(x, mem, ffn1_norm, ffn1_gate, ffn1_up, ffn1_down, mix_norm, w_in, lb_param, hgrn_out_norm, conv_w, w_out, xattn_norm, mem_norm, w_q_mem, w_kv_mem, w_o_mem, ffn2_norm, ffn2_gate, ffn2_up, ffn2_down, final_norm):
    batch, seq, _ = x.shape
    depth = ffn1_norm.shape[0]
    assert depth == 1 and seq % TILE_A == 0 and seq % TILE_C == 0
    l = 0
    x2d = x.reshape(batch * seq, D_MODEL)
    vec = lambda a: a.reshape(1, -1)

    x1, yh, yc = _stage_ab(
        x2d, vec(ffn1_norm[l]), ffn1_gate, ffn1_up, ffn1_down, vec(mix_norm[l]), w_in,
        lb_param, conv_w[l].T, _chunk_tri3(), vec(hgrn_out_norm[l]), l=l, seq=seq)

    out = _stage_c(x1, yh, yc, mem.reshape(batch * MEM_LEN, D_MODEL), vec(mem_norm[l]), w_out,
                   vec(xattn_norm[l]), w_q_mem, w_o_mem, w_kv_mem, vec(ffn2_norm[l]),
                   ffn2_gate, ffn2_up, ffn2_down, vec(final_norm), l=l, seq=seq)
    return out.reshape(batch, seq, D_MODEL)
```

```python
import functools
import math

import jax
import jax.numpy as jnp
from jax import lax
from jax.experimental import pallas as pl
from jax.experimental.pallas import tpu as pltpu

F32 = jnp.float32
BF16 = jnp.bfloat16

D_MODEL = 1024
HGRN_W = 512
CONV_W = 512
HGRN_HEADS = 4
HGRN_DK = 128
HGRN_DV = 128
HGRN_F = HGRN_HEADS * HGRN_DK
CONV_K = 3
CHUNK = 64
MEM_LEN = 256
MEM_HEADS = 4
MEM_HD = D_MODEL // MEM_HEADS
D_FF = int(math.ceil(8 * D_MODEL / 3 / 256) * 256)
EPS = 1e-6
LOG2_E = 1.4426950408889634
N_IN_PIECES = 7
PIECE_W = 512

FF_CHUNK = 256
N_FF_CHUNKS = D_FF // FF_CHUNK
SUBLANES = 8
DIAG_BLOCK = SUBLANES
LEVEL_BLOCKS = (8, 16, 32)

CAST_STEPS = 8
TILE_A = 512
TILE_C = 512
VMEM_LIMIT_BYTES = 60 * 1024 * 1024


def _dot(a, b):
    return jnp.dot(a, b, preferred_element_type=F32)


def _dot_nt(a, b):
    return lax.dot_general(a, b, (((1,), (1,)), ((), ())), preferred_element_type=F32)


def _rms(x, g):
    ms = jnp.mean(x * x, axis=-1, keepdims=True)
    return x * lax.rsqrt(ms + EPS) * g


def _silu(x):
    return x * jax.nn.sigmoid(x)


def _split3(x):
    hi = x.astype(BF16)
    r1 = x - hi.astype(F32)
    mid = r1.astype(BF16)
    r2 = r1 - mid.astype(F32)
    lo = r2.astype(BF16)
    return hi, mid, lo


SIDE_GROUP = 12
K_TILE = 256


def _swiglu_into(h_ref, wg_ref, wu_ref, wd_ref, acc_ref, side_work=None, n_side_steps=0):
    n_slots = N_FF_CHUNKS * 3 * (D_MODEL // K_TILE)
    slot_counter = [0, 0]

    def pull():
        slot_counter[0] += 1
        if slot_counter[0] % SIDE_GROUP:
            return
        due = -(-n_side_steps * slot_counter[0] // n_slots)
        while slot_counter[1] < due:
            next(side_work, None)
            slot_counter[1] += 1

    def dot_ktiles(w_ref, cols):
        out = None
        for kt in range(D_MODEL // K_TILE):
            ks = slice(kt * K_TILE, (kt + 1) * K_TILE)
            part = _dot(h_ref[:, ks], w_ref[ks, cols])
            out = part if out is None else out + part
            pull()
        return out

    for c in range(N_FF_CHUNKS):
        cols = slice(c * FF_CHUNK, (c + 1) * FF_CHUNK)
        g = dot_ktiles(wg_ref, cols)
        u = dot_ktiles(wu_ref, cols)
        a = (_silu(g) * u).astype(BF16)
        for nt in range(D_MODEL // K_TILE):
            ns = slice(nt * K_TILE, (nt + 1) * K_TILE)
            d = _dot(a, wd_ref[cols, ns])
            if c == 0:
                acc_ref[:, ns] = d
            else:
                acc_ref[:, ns] += d
            pull()


def _chunk_masks():
    row = lax.broadcasted_iota(jnp.int32, (CHUNK, CHUNK), 0)
    col = lax.broadcasted_iota(jnp.int32, (CHUNK, CHUNK), 1)
    diag_mask = (row // DIAG_BLOCK == col // DIAG_BLOCK) & (col <= row)
    rk = lax.broadcasted_iota(jnp.int32, (CHUNK, HGRN_DK), 0)
    level_masks, odd_rows = [], []
    for bs in LEVEL_BLOCKS:
        level_masks.append((row // (2 * bs) == col // (2 * bs))
                           & ((row // bs) % 2 == 1) & ((col // bs) % 2 == 0))
        odd_rows.append((rk // bs) % 2 == 1)
    nblk = CHUNK // DIAG_BLOCK
    lane = lax.broadcasted_iota(jnp.int32, (nblk, DIAG_BLOCK, CHUNK), 2)
    blk = lax.broadcasted_iota(jnp.int32, (nblk, DIAG_BLOCK, CHUNK), 0)
    lane_blk = lane - blk * DIAG_BLOCK
    return diag_mask, level_masks, odd_rows, lane_blk


HGRN_UNIT_STEPS = 6 + len(LEVEL_BLOCKS) + DIAG_BLOCK

def _hgrn_unit(c, h, masks, q_s, k_s, b_s, v_s, g_s, gh_ref, y_ref, st_ref):
    diag_mask, level_masks, odd_rows, lane_blk = masks
    rows = slice(c * CHUNK, (c + 1) * CHUNK)
    lanes = slice(h * HGRN_DK, (h + 1) * HGRN_DK)
    vl = slice(h * HGRN_DV, (h + 1) * HGRN_DV)
    q = q_s[rows, lanes]
    kbb = k_s[rows, lanes]
    b2 = b_s[rows, lanes]
    v = v_s[rows, vl]
    vb = v.astype(BF16)
    vbt = v.T.astype(BF16)
    b_last = b2[CHUNK - 1:CHUNK, :]
    qb = (q * jnp.exp2(b2)).astype(BF16)
    kd = jnp.exp2(b_last - kbb).astype(BF16)
    yield
    st = st_ref[h]
    o_inter = _dot_nt(qb, st.astype(BF16))
    upd = _dot(vbt, kd)
    yield

    def level_operands(bs, odd):
        npair = CHUNK // (2 * bs)
        r = b2.reshape(npair, 2 * bs, HGRN_DK)[:, bs:bs + 1, :]
        r = jnp.broadcast_to(r, (npair, 2 * bs, HGRN_DK)).reshape(CHUNK, HGRN_DK)
        e = jnp.exp2(jnp.where(odd, b2 - r, r - kbb))
        return (q * e).astype(BF16), e.astype(BF16)

    operands = level_operands(LEVEL_BLOCKS[0], odd_rows[0])
    st_ref[h] = st * jnp.exp2(b_last) + upd
    yield
    level_scores = []
    for nxt in range(1, len(LEVEL_BLOCKS) + 1):
        level_scores.append(_dot_nt(*operands))
        if nxt < len(LEVEL_BLOCKS):
            operands = level_operands(LEVEL_BLOCKS[nxt], odd_rows[nxt])
        yield
    nblk = CHUNK // DIAG_BLOCK
    q3 = q.reshape(nblk, DIAG_BLOCK, HGRN_DK)
    b3 = b2.reshape(nblk, DIAG_BLOCK, HGRN_DK)
    kbb3 = kbb.reshape(nblk, DIAG_BLOCK, HGRN_DK)
    s3 = jnp.zeros((nblk, DIAG_BLOCK, CHUNK), F32)
    for s in range(DIAG_BLOCK):
        tmp = q3 * jnp.exp2(b3 - kbb3[:, s:s + 1, :])
        red = jnp.sum(tmp, axis=-1, keepdims=True)
        s3 = jnp.where(lane_blk == s, red, s3)
        yield
    scores = jnp.where(diag_mask, s3.reshape(CHUNK, CHUNK), 0.0)
    for lmask, ls in zip(level_masks, level_scores):
        scores = jnp.where(lmask, ls, scores)
    scores = scores.astype(BF16)
    yield
    o_intra = _dot(scores, vb)
    yield
    o = o_inter + o_intra
    o = o * lax.rsqrt(jnp.mean(o * o, axis=-1, keepdims=True) + EPS) * gh_ref[:, vl]
    y_ref[rows, vl] = (o * g_s[rows, vl]).astype(y_ref.dtype)
    yield


def _hgrn_pieces(q_s, k_s, b_s, v_s, g_s, gh_ref, y_ref, st_ref):
    masks = _chunk_masks()
    for c in range(q_s.shape[0] // CHUNK):
        units = [_hgrn_unit(c, h, masks, q_s, k_s, b_s, v_s, g_s, gh_ref, y_ref, st_ref)
                 for h in range(HGRN_HEADS)]
        while units:
            for u in list(units):
                try:
                    next(u)
                    yield
                except StopIteration:
                    units.remove(u)


def _cast_weight_rows(i, pairs):
    for w_ref, dst_ref in pairs:
        rb = w_ref.shape[0]
        dst_ref[pl.ds(pl.multiple_of(i * rb, rb), rb), :] = w_ref[...].astype(dst_ref.dtype)


def _weight_block_spec(w, l, steps):
    _, rows, cols = w.shape
    return pl.BlockSpec((None, rows // steps, cols), lambda i: (l, jnp.minimum(i, steps - 1), 0))


def _weight_scratch(w):
    return pltpu.VMEM(w.shape[1:], BF16)


def _inproj_tile(side_work, n_side_steps, x_ref, g1_ref, wg_ref, wu_ref, wd_ref, gm_ref, win_ref, lbp_ref,
                 cw_ref, tri_ref, x1_ref, yc_ref, h_ref, acc_ref, ubuf_ref, q_s, k_s, b_s, v_s, g_s):
    tm = x_ref.shape[0]
    x = x_ref[...]
    h_ref[...] = _rms(x, g1_ref[...]).astype(BF16)
    _swiglu_into(h_ref, wg_ref, wu_ref, wd_ref, acc_ref, side_work, n_side_steps)
    x1 = x + 0.5 * acc_ref[...]
    x1_ref[...] = x1
    h_ref[...] = _rms(x1, gm_ref[...]).astype(BF16)

    def proj(p):
        return _dot(h_ref[...], win_ref[:, p * PIECE_W:(p + 1) * PIECE_W])

    lbp = lbp_ref[...]
    lbe = jnp.exp(lbp - jnp.max(lbp, axis=0, keepdims=True))
    lb = lbe[0:1, :] / jnp.sum(lbe, axis=0, keepdims=True)

    u = proj(5) * proj(6)
    ubuf_ref[SUBLANES:SUBLANES + tm, :] = u
    u1 = ubuf_ref[pl.ds(SUBLANES - 1, tm), :]
    u2 = ubuf_ref[pl.ds(SUBLANES - 2, tm), :]
    cw = cw_ref[...]
    yc_ref[...] = (proj(4) * (cw[0:1, :] * u2 + cw[1:2, :] * u1 + cw[2:3, :] * u)).astype(yc_ref.dtype)
    ubuf_ref[0:SUBLANES, :] = ubuf_ref[tm:tm + SUBLANES, :]

    f = lb + (1.0 - lb) * jax.nn.sigmoid(proj(1))
    log2f = jnp.log(f) * LOG2_E
    hi, mid, lo = _split3(log2f)
    tri3 = tri_ref[...]
    b2 = jnp.concatenate(
        [_dot(tri3, jnp.concatenate([t[c * CHUNK:(c + 1) * CHUNK] for t in (hi, mid, lo)], axis=0))
         for c in range(tm // CHUNK)], axis=0)
    b_s[...] = b2
    k_s[...] = b2 - jnp.log(1.0 - f) * LOG2_E

    q_s[...] = _silu(proj(0)) * (HGRN_DK ** -0.5)
    g_s[...] = _silu(proj(3))
    v_s[...] = proj(2)
    for _ in side_work:
        pass


def _stage_ab_kernel(x_ref, g1_ref, wg_f32, wu_f32, wd_f32, gm_ref, win_f32, lbp_ref, cw_ref,
                     tri_ref, gh_ref,
                     x1_ref, yh_ref, yc_ref,
                     wg_ref, wu_ref, wd_ref, win_ref,
                     h_ref, acc_ref, ubuf_ref, q_s, k_s, b_s, v_s, g_s, st_ref,
                     *, tiles_per_seq, n_tiles):
    i = pl.program_id(0)
    j = i - CAST_STEPS
    staging = (q_s, k_s, b_s, v_s, g_s)

    @pl.when(i < CAST_STEPS)
    def _():
        _cast_weight_rows(i, ((wg_f32, wg_ref), (wu_f32, wu_ref), (wd_f32, wd_ref), (win_f32, win_ref)))

    @pl.when(j == 0)
    def _():
        for ref in staging:
            ref[...] = jnp.zeros(ref.shape, F32)

    @pl.when((j + tiles_per_seq - 1) % tiles_per_seq == 0)
    def _():
        st_ref[...] = jnp.zeros_like(st_ref)

    @pl.when(j % tiles_per_seq == 0)
    def _():
        ubuf_ref[0:SUBLANES, :] = jnp.zeros((SUBLANES, CONV_W), F32)

    @pl.when((j >= 0) & (j < n_tiles))
    def _():
        hgrn = _hgrn_pieces(*staging, gh_ref, yh_ref, st_ref)
        _inproj_tile(hgrn, (TILE_A // CHUNK) * HGRN_HEADS * HGRN_UNIT_STEPS, x_ref, g1_ref, wg_ref, wu_ref, wd_ref, gm_ref, win_ref, lbp_ref,
                     cw_ref, tri_ref, x1_ref, yc_ref, h_ref, acc_ref, ubuf_ref, *staging)

    @pl.when(j == n_tiles)
    def _():
        for _ in _hgrn_pieces(*staging, gh_ref, yh_ref, st_ref):
            pass


def _const_spec(shape):
    nd = len(shape)
    return pl.BlockSpec(shape, lambda *_: (0,) * nd, pipeline_mode=pl.Buffered(1))


def _stage_ab(x2d, g1, wg, wu, wd, gm, win, lbp, cw, tri, gh, *, l, seq):
    t = x2d.shape[0]
    tm = TILE_A
    n_tiles = t // tm
    tile = lambda i: jnp.clip(i - CAST_STEPS, 0, n_tiles - 1)
    cur = lambda w: pl.BlockSpec((tm, w), lambda i: (tile(i), 0))
    prev = lambda w: pl.BlockSpec((tm, w), lambda i: (tile(i - 1), 0))
    wspec = lambda w: _weight_block_spec(w, l, CAST_STEPS)
    cspec = lambda c: _const_spec(c.shape)
    stage = pltpu.VMEM((tm, PIECE_W), F32)
    return pl.pallas_call(
        functools.partial(_stage_ab_kernel, tiles_per_seq=seq // tm, n_tiles=n_tiles),
        grid=(CAST_STEPS + n_tiles + 1,),
        in_specs=[cur(D_MODEL), cspec(g1), wspec(wg), wspec(wu), wspec(wd), cspec(gm), wspec(win),
                  cspec(lbp), cspec(cw), cspec(tri), cspec(gh)],
        out_specs=[cur(D_MODEL), prev(HGRN_W), cur(CONV_W)],
        out_shape=[jax.ShapeDtypeStruct((t, D_MODEL), F32), jax.ShapeDtypeStruct((t, HGRN_W), BF16),
                   jax.ShapeDtypeStruct((t, CONV_W), BF16)],
        scratch_shapes=[_weight_scratch(wg), _weight_scratch(wu), _weight_scratch(wd), _weight_scratch(win),
                        pltpu.VMEM((tm, D_MODEL), BF16), pltpu.VMEM((tm, D_MODEL), F32),
                        pltpu.VMEM((tm + SUBLANES, CONV_W), F32),
                        stage, stage, stage, stage, stage,
                        pltpu.VMEM((HGRN_HEADS, HGRN_DV, HGRN_DK), F32)],
        compiler_params=pltpu.CompilerParams(dimension_semantics=("arbitrary",),
                                             vmem_limit_bytes=VMEM_LIMIT_BYTES),
        name="stage_ab_ffn1_inproj_hgrn2",
    )(x2d, g1, wg, wu, wd, gm, win, lbp, cw, tri, gh)


def _stage_c_kernel(x1_ref, yh_ref, yc_ref, mem_ref, gmem_ref, wout_f32, gx_ref, wq_f32, wo_f32, wkv_f32,
                    g2_ref, wg_f32, wu_f32, wd_f32, gf_ref,
                    out_ref,
                    wout_ref, wq_ref, wo_ref, wkv_ref, wg_ref, wu_ref, wd_ref,
                    km_ref, vm_ref, h_ref, acc_ref, att_ref, *, tiles_per_seq):
    i = pl.program_id(0)
    j = i - CAST_STEPS

    @pl.when(i < CAST_STEPS)
    def _():
        _cast_weight_rows(i, ((wout_f32, wout_ref), (wq_f32, wq_ref), (wo_f32, wo_ref), (wkv_f32, wkv_ref),
                              (wg_f32, wg_ref), (wu_f32, wu_ref), (wd_f32, wd_ref)))

    @pl.when((j >= 0) & (j % tiles_per_seq == 0))
    def _():
        mn = _rms(mem_ref[...], gmem_ref[...]).astype(BF16)
        kv = _dot(mn, wkv_ref[...])
        km_ref[...] = kv[:, :D_MODEL].astype(BF16)
        vm_ref[...] = kv[:, D_MODEL:].astype(BF16)

    @pl.when(j >= 0)
    def _():
        x2 = (x1_ref[...]
              + _dot(yh_ref[...], wout_ref[0:HGRN_W, :])
              + _dot(yc_ref[...], wout_ref[HGRN_W:HGRN_W + CONV_W, :]))
        hq = _rms(x2, gx_ref[...]).astype(BF16)
        qm = _dot(hq, wq_ref[...]) * (MEM_HD ** -0.5)
        head = lambda h: slice(h * MEM_HD, (h + 1) * MEM_HD)
        scores = lambda h: _dot_nt(qm[:, head(h)].astype(BF16), km_ref[:, head(h)])
        sc = scores(0)
        for h in range(MEM_HEADS):
            sc_next = scores(h + 1) if h + 1 < MEM_HEADS else None
            e = jnp.exp(sc - jnp.max(sc, axis=-1, keepdims=True))
            pv = _dot(e.astype(BF16), vm_ref[:, head(h)])
            att_ref[:, head(h)] = (pv / jnp.sum(e, axis=-1, keepdims=True)).astype(BF16)
            sc = sc_next
        x3 = x2 + _dot(att_ref[...], wo_ref[...])
        h_ref[...] = _rms(x3, g2_ref[...]).astype(BF16)
        _swiglu_into(h_ref, wg_ref, wu_ref, wd_ref, acc_ref)
        x4 = x3 + 0.5 * acc_ref[...]
        out_ref[...] = _rms(x4, gf_ref[...])


def _stage_c(x1, yh, yc, mem2d, gmem, wout, gx, wq, wo, wkv, g2, wg, wu, wd, gf, *, l, seq):
    t = x1.shape[0]
    tm = TILE_C
    tiles_per_seq = seq // tm
    tile = lambda i: jnp.maximum(i - CAST_STEPS, 0)
    row = lambda w: pl.BlockSpec((tm, w), lambda i: (tile(i), 0))
    memblk = pl.BlockSpec((MEM_LEN, D_MODEL), lambda i: (tile(i) // tiles_per_seq, 0))
    wspec = lambda w: _weight_block_spec(w, l, CAST_STEPS)
    cspec = lambda c: _const_spec(c.shape)
    weights = (wout, wq, wo, wkv, wg, wu, wd)
    return pl.pallas_call(
        functools.partial(_stage_c_kernel, tiles_per_seq=tiles_per_seq),
        grid=(CAST_STEPS + t // tm,),
        in_specs=[row(D_MODEL), row(HGRN_W), row(CONV_W), memblk, cspec(gmem), wspec(wout), cspec(gx),
                  wspec(wq), wspec(wo), wspec(wkv), cspec(g2), wspec(wg), wspec(wu), wspec(wd), cspec(gf)],
        out_specs=row(D_MODEL),
        out_shape=jax.ShapeDtypeStruct((t, D_MODEL), F32),
        scratch_shapes=[_weight_scratch(w) for w in weights] + [
            pltpu.VMEM((MEM_LEN, D_MODEL), BF16), pltpu.VMEM((MEM_LEN, D_MODEL), BF16),
            pltpu.VMEM((tm, D_MODEL), BF16), pltpu.VMEM((tm, D_MODEL), F32),
            pltpu.VMEM((tm, D_MODEL), BF16)],
        compiler_params=pltpu.CompilerParams(dimension_semantics=("arbitrary",),
                                             vmem_limit_bytes=VMEM_LIMIT_BYTES),
        name="stage_c_outproj_xattn_ffn2",
    )(x1, yh, yc, mem2d, gmem, wout, gx, wq, wo, wkv, g2, wg, wu, wd, gf)


def _chunk_tri3():
    r = lax.broadcasted_iota(jnp.int32, (CHUNK, 3 * CHUNK), 0)
    c = lax.broadcasted_iota(jnp.int32, (CHUNK, 3 * CHUNK), 1)
    return (c % CHUNK <= r).astype(BF16)


def kernel(x, mem, ffn1_norm, ffn1_gate, ffn1_up, ffn1_down, mix_norm, w_in, lb_param, hgrn_out_norm, conv_w, w_out, xattn_norm, mem_norm, w_q_mem, w_kv_mem, w_o_mem, ffn2_norm, ffn2_gate, ffn2_up, ffn2_down, final_norm):
    batch, seq, _ = x.shape
    depth = ffn1_norm.shape[0]
    assert depth == 1 and seq % TILE_A == 0 and seq % TILE_C == 0
    l = 0
    x2d = x.reshape(batch * seq, D_MODEL)
    vec = lambda a: a.reshape(1, -1)

    x1, yh, yc = _stage_ab(
        x2d, vec(ffn1_norm[l]), ffn1_gate, ffn1_up, ffn1_down, vec(mix_norm[l]), w_in,
        lb_param, conv_w[l].T, _chunk_tri3(), vec(hgrn_out_norm[l]), l=l, seq=seq)

    out = _stage_c(x1, yh, yc, mem.reshape(batch * MEM_LEN, D_MODEL), vec(mem_norm[l]), w_out,
                   vec(xattn_norm[l]), w_q_mem, w_o_mem, w_kv_mem, vec(ffn2_norm[l]),
                   ffn2_gate, ffn2_up, ffn2_down, vec(final_norm), l=l, seq=seq)
    return out.reshape(batch, seq, D_MODEL)
```

```python
import functools
import math

import jax
import jax.numpy as jnp
from jax import lax
from jax.experimental import pallas as pl
from jax.experimental.pallas import tpu as pltpu

F32 = jnp.float32
BF16 = jnp.bfloat16

D_MODEL = 1024
HGRN_W = 512
CONV_W = 512
HGRN_HEADS = 4
HGRN_DK = 128
HGRN_DV = 128
HGRN_F = HGRN_HEADS * HGRN_DK
CONV_K = 3
CHUNK = 64
MEM_LEN = 256
MEM_HEADS = 4
MEM_HD = D_MODEL // MEM_HEADS
D_FF = int(math.ceil(8 * D_MODEL / 3 / 256) * 256)
EPS = 1e-6
LOG2_E = 1.4426950408889634
N_IN_PIECES = 7
PIECE_W = 512

FF_CHUNK = 256
N_FF_CHUNKS = D_FF // FF_CHUNK
SUBLANES = 8
DIAG_BLOCK = SUBLANES
LEVEL_BLOCKS = (8, 16, 32)

CAST_STEPS = 8
TILE_A = 512
TILE_C = 512
VMEM_LIMIT_BYTES = 60 * 1024 * 1024


def _dot(a, b):
    return jnp.dot(a, b, preferred_element_type=F32)


def _dot_nt(a, b):
    return lax.dot_general(a, b, (((1,), (1,)), ((), ())), preferred_element_type=F32)


def _rms(x, g):
    ms = jnp.mean(x * x, axis=-1, keepdims=True)
    return x * lax.rsqrt(ms + EPS) * g


def _silu(x):
    return x * jax.nn.sigmoid(x)


def _split3(x):
    hi = x.astype(BF16)
    r1 = x - hi.astype(F32)
    mid = r1.astype(BF16)
    r2 = r1 - mid.astype(F32)
    lo = r2.astype(BF16)
    return hi, mid, lo


SIDE_GROUP = 12
K_TILE = 256


def _swiglu_into(h_ref, wg_ref, wu_ref, wd_ref, acc_ref, side_work=None, n_side_steps=0):
    n_slots = N_FF_CHUNKS * 3 * (D_MODEL // K_TILE)
    slot_counter = [0, 0]

    def pull():
        slot_counter[0] += 1
        if slot_counter[0] % SIDE_GROUP:
            return
        due = -(-n_side_steps * slot_counter[0] // n_slots)
        while slot_counter[1] < due:
            next(side_work, None)
            slot_counter[1] += 1

    def dot_ktiles(w_ref, cols):
        out = None
        for kt in range(D_MODEL // K_TILE):
            ks = slice(kt * K_TILE, (kt + 1) * K_TILE)
            part = _dot(h_ref[:, ks], w_ref[ks, cols])
            out = part if out is None else out + part
            pull()
        return out

    for c in range(N_FF_CHUNKS):
        cols = slice(c * FF_CHUNK, (c + 1) * FF_CHUNK)
        g = dot_ktiles(wg_ref, cols)
        u = dot_ktiles(wu_ref, cols)
        a = (_silu(g) * u).astype(BF16)
        for nt in range(D_MODEL // K_TILE):
            ns = slice(nt * K_TILE, (nt + 1) * K_TILE)
            d = _dot(a, wd_ref[cols, ns])
            if c == 0:
                acc_ref[:, ns] = d
            else:
                acc_ref[:, ns] += d
            pull()


def _chunk_masks():
    row = lax.broadcasted_iota(jnp.int32, (CHUNK, CHUNK), 0)
    col = lax.broadcasted_iota(jnp.int32, (CHUNK, CHUNK), 1)
    diag_mask = (row // DIAG_BLOCK == col // DIAG_BLOCK) & (col <= row)
    rk = lax.broadcasted_iota(jnp.int32, (CHUNK, HGRN_DK), 0)
    level_masks, odd_rows = [], []
    for bs in LEVEL_BLOCKS:
        level_masks.append((row // (2 * bs) == col // (2 * bs))
                           & ((row // bs) % 2 == 1) & ((col // bs) % 2 == 0))
        odd_rows.append((rk // bs) % 2 == 1)
    nblk = CHUNK // DIAG_BLOCK
    lane = lax.broadcasted_iota(jnp.int32, (nblk, DIAG_BLOCK, CHUNK), 2)
    blk = lax.broadcasted_iota(jnp.int32, (nblk, DIAG_BLOCK, CHUNK), 0)
    lane_blk = lane - blk * DIAG_BLOCK
    return diag_mask, level_masks, odd_rows, lane_blk


HGRN_UNIT_STEPS = 3 + len(LEVEL_BLOCKS) + DIAG_BLOCK

def _hgrn_unit(c, h, masks, q_s, k_s, b_s, v_s, g_s, gh_ref, y_ref, st_ref):
    diag_mask, level_masks, odd_rows, lane_blk = masks
    rows = slice(c * CHUNK, (c + 1) * CHUNK)
    lanes = slice(h * HGRN_DK, (h + 1) * HGRN_DK)
    vl = slice(h * HGRN_DV, (h + 1) * HGRN_DV)
    q = q_s[rows, lanes]
    kbb = k_s[rows, lanes]
    b2 = b_s[rows, lanes]
    v = v_s[rows, vl]
    vb = v.astype(BF16)
    vbt = v.T.astype(BF16)
    b_last = b2[CHUNK - 1:CHUNK, :]
    qb = (q * jnp.exp2(b2)).astype(BF16)
    kd = jnp.exp2(b_last - kbb).astype(BF16)
    st = st_ref[h]
    o_inter = _dot_nt(qb, st.astype(BF16))
    upd = _dot(vbt, kd)
    yield

    def level_operands(bs, odd):
        npair = CHUNK // (2 * bs)
        r = b2.reshape(npair, 2 * bs, HGRN_DK)[:, bs:bs + 1, :]
        r = jnp.broadcast_to(r, (npair, 2 * bs, HGRN_DK)).reshape(CHUNK, HGRN_DK)
        e = jnp.exp2(jnp.where(odd, b2 - r, r - kbb))
        return (q * e).astype(BF16), e.astype(BF16)

    st_ref[h] = st * jnp.exp2(b_last) + upd
    level_scores = []
    for bs, odd in zip(LEVEL_BLOCKS, odd_rows):
        level_scores.append(_dot_nt(*level_operands(bs, odd)))
        yield
    nblk = CHUNK // DIAG_BLOCK
    q3 = q.reshape(nblk, DIAG_BLOCK, HGRN_DK)
    b3 = b2.reshape(nblk, DIAG_BLOCK, HGRN_DK)
    kbb3 = kbb.reshape(nblk, DIAG_BLOCK, HGRN_DK)
    s3 = jnp.zeros((nblk, DIAG_BLOCK, CHUNK), F32)
    for s in range(DIAG_BLOCK):
        tmp = q3 * jnp.exp2(b3 - kbb3[:, s:s + 1, :])
        red = jnp.sum(tmp, axis=-1, keepdims=True)
        s3 = jnp.where(lane_blk == s, red, s3)
        yield
    scores = jnp.where(diag_mask, s3.reshape(CHUNK, CHUNK), 0.0)
    for lmask, ls in zip(level_masks, level_scores):
        scores = jnp.where(lmask, ls, scores)
    o_intra = _dot(scores.astype(BF16), vb)
    yield
    o = o_inter + o_intra
    o = o * lax.rsqrt(jnp.mean(o * o, axis=-1, keepdims=True) + EPS) * gh_ref[:, vl]
    y_ref[rows, vl] = (o * g_s[rows, vl]).astype(y_ref.dtype)
    yield


def _hgrn_pieces(q_s, k_s, b_s, v_s, g_s, gh_ref, y_ref, st_ref):
    masks = _chunk_masks()
    for c in range(q_s.shape[0] // CHUNK):
        units = [_hgrn_unit(c, h, masks, q_s, k_s, b_s, v_s, g_s, gh_ref, y_ref, st_ref)
                 for h in range(HGRN_HEADS)]
        while units:
            for u in list(units):
                try:
                    next(u)
                    yield
                except StopIteration:
                    units.remove(u)


def _cast_weight_rows(i, pairs):
    for w_ref, dst_ref in pairs:
        rb = w_ref.shape[0]
        dst_ref[pl.ds(pl.multiple_of(i * rb, rb), rb), :] = w_ref[...].astype(dst_ref.dtype)


def _weight_block_spec(w, l, steps):
    _, rows, cols = w.shape
    return pl.BlockSpec((None, rows // steps, cols), lambda i: (l, jnp.minimum(i, steps - 1), 0))


def _weight_scratch(w):
    return pltpu.VMEM(w.shape[1:], BF16)


def _inproj_tile(side_work, n_side_steps, x_ref, g1_ref, wg_ref, wu_ref, wd_ref, gm_ref, win_ref, lbp_ref,
                 cw_ref, tri_ref, x1_ref, yc_ref, h_ref, acc_ref, ubuf_ref, q_s, k_s, b_s, v_s, g_s):
    tm = x_ref.shape[0]
    x = x_ref[...]
    h_ref[...] = _rms(x, g1_ref[...]).astype(BF16)
    _swiglu_into(h_ref, wg_ref, wu_ref, wd_ref, acc_ref, side_work, n_side_steps)
    x1 = x + 0.5 * acc_ref[...]
    x1_ref[...] = x1
    h_ref[...] = _rms(x1, gm_ref[...]).astype(BF16)

    def proj(p):
        return _dot(h_ref[...], win_ref[:, p * PIECE_W:(p + 1) * PIECE_W])

    lbp = lbp_ref[...]
    lbe = jnp.exp(lbp - jnp.max(lbp, axis=0, keepdims=True))
    lb = lbe[0:1, :] / jnp.sum(lbe, axis=0, keepdims=True)

    u = proj(5) * proj(6)
    ubuf_ref[SUBLANES:SUBLANES + tm, :] = u
    u1 = ubuf_ref[pl.ds(SUBLANES - 1, tm), :]
    u2 = ubuf_ref[pl.ds(SUBLANES - 2, tm), :]
    cw = cw_ref[...]
    yc_ref[...] = (proj(4) * (cw[0:1, :] * u2 + cw[1:2, :] * u1 + cw[2:3, :] * u)).astype(yc_ref.dtype)
    ubuf_ref[0:SUBLANES, :] = ubuf_ref[tm:tm + SUBLANES, :]

    f = lb + (1.0 - lb) * jax.nn.sigmoid(proj(1))
    log2f = jnp.log(f) * LOG2_E
    hi, mid, lo = _split3(log2f)
    tri3 = tri_ref[...]
    b2 = jnp.concatenate(
        [_dot(tri3, jnp.concatenate([t[c * CHUNK:(c + 1) * CHUNK] for t in (hi, mid, lo)], axis=0))
         for c in range(tm // CHUNK)], axis=0)
    b_s[...] = b2
    k_s[...] = b2 - jnp.log(1.0 - f) * LOG2_E

    q_s[...] = _silu(proj(0)) * (HGRN_DK ** -0.5)
    g_s[...] = _silu(proj(3))
    v_s[...] = proj(2)
    for _ in side_work:
        pass


def _stage_ab_kernel(x_ref, g1_ref, wg_f32, wu_f32, wd_f32, gm_ref, win_f32, lbp_ref, cw_ref,
                     tri_ref, gh_ref,
                     x1_ref, yh_ref, yc_ref,
                     wg_ref, wu_ref, wd_ref, win_ref,
                     h_ref, acc_ref, ubuf_ref, q_s, k_s, b_s, v_s, g_s, st_ref,
                     *, tiles_per_seq, n_tiles):
    i = pl.program_id(0)
    j = i - CAST_STEPS
    staging = (q_s, k_s, b_s, v_s, g_s)

    @pl.when(i < CAST_STEPS)
    def _():
        _cast_weight_rows(i, ((wg_f32, wg_ref), (wu_f32, wu_ref), (wd_f32, wd_ref), (win_f32, win_ref)))

    @pl.when(j == 0)
    def _():
        for ref in staging:
            ref[...] = jnp.zeros(ref.shape, F32)

    @pl.when((j + tiles_per_seq - 1) % tiles_per_seq == 0)
    def _():
        st_ref[...] = jnp.zeros_like(st_ref)

    @pl.when(j % tiles_per_seq == 0)
    def _():
        ubuf_ref[0:SUBLANES, :] = jnp.zeros((SUBLANES, CONV_W), F32)

    @pl.when((j >= 0) & (j < n_tiles))
    def _():
        hgrn = _hgrn_pieces(*staging, gh_ref, yh_ref, st_ref)
        _inproj_tile(hgrn, (TILE_A // CHUNK) * HGRN_HEADS * HGRN_UNIT_STEPS, x_ref, g1_ref, wg_ref, wu_ref, wd_ref, gm_ref, win_ref, lbp_ref,
                     cw_ref, tri_ref, x1_ref, yc_ref, h_ref, acc_ref, ubuf_ref, *staging)

    @pl.when(j == n_tiles)
    def _():
        for _ in _hgrn_pieces(*staging, gh_ref, yh_ref, st_ref):
            pass


def _const_spec(shape):
    nd = len(shape)
    return pl.BlockSpec(shape, lambda *_: (0,) * nd, pipeline_mode=pl.Buffered(1))


def _stage_ab(x2d, g1, wg, wu, wd, gm, win, lbp, cw, tri, gh, *, l, seq):
    t = x2d.shape[0]
    tm = TILE_A
    n_tiles = t // tm
    tile = lambda i: jnp.clip(i - CAST_STEPS, 0, n_tiles - 1)
    cur = lambda w: pl.BlockSpec((tm, w), lambda i: (tile(i), 0))
    prev = lambda w: pl.BlockSpec((tm, w), lambda i: (tile(i - 1), 0))
    wspec = lambda w: _weight_block_spec(w, l, CAST_STEPS)
    cspec = lambda c: _const_spec(c.shape)
    stage = pltpu.VMEM((tm, PIECE_W), F32)
    return pl.pallas_call(
        functools.partial(_stage_ab_kernel, tiles_per_seq=seq // tm, n_tiles=n_tiles),
        grid=(CAST_STEPS + n_tiles + 1,),
        in_specs=[cur(D_MODEL), cspec(g1), wspec(wg), wspec(wu), wspec(wd), cspec(gm), wspec(win),
                  cspec(lbp), cspec(cw), cspec(tri), cspec(gh)],
        out_specs=[cur(D_MODEL), prev(HGRN_W), cur(CONV_W)],
        out_shape=[jax.ShapeDtypeStruct((t, D_MODEL), F32), jax.ShapeDtypeStruct((t, HGRN_W), BF16),
                   jax.ShapeDtypeStruct((t, CONV_W), BF16)],
        scratch_shapes=[_weight_scratch(wg), _weight_scratch(wu), _weight_scratch(wd), _weight_scratch(win),
                        pltpu.VMEM((tm, D_MODEL), BF16), pltpu.VMEM((tm, D_MODEL), F32),
                        pltpu.VMEM((tm + SUBLANES, CONV_W), F32),
                        stage, stage, stage, stage, stage,
                        pltpu.VMEM((HGRN_HEADS, HGRN_DV, HGRN_DK), F32)],
        compiler_params=pltpu.CompilerParams(dimension_semantics=("arbitrary",),
                                             vmem_limit_bytes=VMEM_LIMIT_BYTES),
        name="stage_ab_ffn1_inproj_hgrn2",
    )(x2d, g1, wg, wu, wd, gm, win, lbp, cw, tri, gh)


def _stage_c_kernel(x1_ref, yh_ref, yc_ref, mem_ref, gmem_ref, wout_f32, gx_ref, wq_f32, wo_f32, wkv_f32,
                    g2_ref, wg_f32, wu_f32, wd_f32, gf_ref,
                    out_ref,
                    wout_ref, wq_ref, wo_ref, wkv_ref, wg_ref, wu_ref, wd_ref,
                    km_ref, vm_ref, h_ref, acc_ref, att_ref, *, tiles_per_seq):
    i = pl.program_id(0)
    j = i - CAST_STEPS

    @pl.when(i < CAST_STEPS)
    def _():
        _cast_weight_rows(i, ((wout_f32, wout_ref), (wq_f32, wq_ref), (wo_f32, wo_ref), (wkv_f32, wkv_ref),
                              (wg_f32, wg_ref), (wu_f32, wu_ref), (wd_f32, wd_ref)))

    @pl.when((j >= 0) & (j % tiles_per_seq == 0))
    def _():
        mn = _rms(mem_ref[...], gmem_ref[...]).astype(BF16)
        kv = _dot(mn, wkv_ref[...])
        km_ref[...] = kv[:, :D_MODEL].astype(BF16)
        vm_ref[...] = kv[:, D_MODEL:].astype(BF16)

    @pl.when(j >= 0)
    def _():
        x2 = (x1_ref[...]
              + _dot(yh_ref[...], wout_ref[0:HGRN_W, :])
              + _dot(yc_ref[...], wout_ref[HGRN_W:HGRN_W + CONV_W, :]))
        hq = _rms(x2, gx_ref[...]).astype(BF16)
        qm = _dot(hq, wq_ref[...]) * (MEM_HD ** -0.5)
        head = lambda h: slice(h * MEM_HD, (h + 1) * MEM_HD)
        scores = lambda h: _dot_nt(qm[:, head(h)].astype(BF16), km_ref[:, head(h)])
        sc = scores(0)
        for h in range(MEM_HEADS):
            sc_next = scores(h + 1) if h + 1 < MEM_HEADS else None
            e = jnp.exp(sc - jnp.max(sc, axis=-1, keepdims=True))
            pv = _dot(e.astype(BF16), vm_ref[:, head(h)])
            att_ref[:, head(h)] = (pv / jnp.sum(e, axis=-1, keepdims=True)).astype(BF16)
            sc = sc_next
        x3 = x2 + _dot(att_ref[...], wo_ref[...])
        h_ref[...] = _rms(x3, g2_ref[...]).astype(BF16)
        _swiglu_into(h_ref, wg_ref, wu_ref, wd_ref, acc_ref)
        x4 = x3 + 0.5 * acc_ref[...]
        out_ref[...] = _rms(x4, gf_ref[...])


def _stage_c(x1, yh, yc, mem2d, gmem, wout, gx, wq, wo, wkv, g2, wg, wu, wd, gf, *, l, seq):
    t = x1.shape[0]
    tm = TILE_C
    tiles_per_seq = seq // tm
    tile = lambda i: jnp.maximum(i - CAST_STEPS, 0)
    row = lambda w: pl.BlockSpec((tm, w), lambda i: (tile(i), 0))
    memblk = pl.BlockSpec((MEM_LEN, D_MODEL), lambda i: (tile(i) // tiles_per_seq, 0))
    wspec = lambda w: _weight_block_spec(w, l, CAST_STEPS)
    cspec = lambda c: _const_spec(c.shape)
    weights = (wout, wq, wo, wkv, wg, wu, wd)
    return pl.pallas_call(
        functools.partial(_stage_c_kernel, tiles_per_seq=tiles_per_seq),
        grid=(CAST_STEPS + t // tm,),
        in_specs=[row(D_MODEL), row(HGRN_W), row(CONV_W), memblk, cspec(gmem), wspec(wout), cspec(gx),
                  wspec(wq), wspec(wo), wspec(wkv), cspec(g2), wspec(wg), wspec(wu), wspec(wd), cspec(gf)],
        out_specs=row(D_MODEL),
        out_shape=jax.ShapeDtypeStruct((t, D_MODEL), F32),
        scratch_shapes=[_weight_scratch(w) for w in weights] + [
            pltpu.VMEM((MEM_LEN, D_MODEL), BF16), pltpu.VMEM((MEM_LEN, D_MODEL), BF16),
            pltpu.VMEM((tm, D_MODEL), BF16), pltpu.VMEM((tm, D_MODEL), F32),
            pltpu.VMEM((tm, D_MODEL), BF16)],
        compiler_params=pltpu.CompilerParams(dimension_semantics=("arbitrary",),
                                             vmem_limit_bytes=VMEM_LIMIT_BYTES),
        name="stage_c_outproj_xattn_ffn2",
    )(x1, yh, yc, mem2d, gmem, wout, gx, wq, wo, wkv, g2, wg, wu, wd, gf)


def _chunk_tri3():
    r = lax.broadcasted_iota(jnp.int32, (CHUNK, 3 * CHUNK), 0)
    c = lax.broadcasted_iota(jnp.int32, (CHUNK, 3 * CHUNK), 1)
    return (c % CHUNK <= r).astype(BF16)


def kernel(x, mem, ffn1_norm, ffn1_gate, ffn1_up, ffn1_down, mix_norm, w_in, lb_param, hgrn_out_norm, conv_w, w_out, xattn_norm, mem_norm, w_q_mem, w_kv_mem, w_o_mem, ffn2_norm, ffn2_gate, ffn2_up, ffn2_down, final_norm):
    batch, seq, _ = x.shape
    depth = ffn1_norm.shape[0]
    assert depth == 1 and seq % TILE_A == 0 and seq % TILE_C == 0
    l = 0
    x2d = x.reshape(batch * seq, D_MODEL)
    vec = lambda a: a.reshape(1, -1)

    x1, yh, yc = _stage_ab(
        x2d, vec(ffn1_norm[l]), ffn1_gate, ffn1_up, ffn1_down, vec(mix_norm[l]), w_in,
        lb_param, conv_w[l].T, _chunk_tri3(), vec(hgrn_out_norm[l]), l=l, seq=seq)

    out = _stage_c(x1, yh, yc, mem.reshape(batch * MEM_LEN, D_MODEL), vec(mem_norm[l]), w_out,
                   vec(xattn_norm[l]), w_q_mem, w_o_mem, w_kv_mem, vec(ffn2_norm[l]),
                   ffn2_gate, ffn2_up, ffn2_down, vec(final_norm), l=l, seq=seq)
    return out.reshape(batch, seq, D_MODEL)
```

```python
import functools
import math

import jax
import jax.numpy as jnp
from jax import lax
from jax.experimental import pallas as pl
from jax.experimental.pallas import tpu as pltpu

F32 = jnp.float32
BF16 = jnp.bfloat16

D_MODEL = 1024
HGRN_W = 512
CONV_W = 512
HGRN_HEADS = 4
HGRN_DK = 128
HGRN_DV = 128
HGRN_F = HGRN_HEADS * HGRN_DK
CONV_K = 3
CHUNK = 64
MEM_LEN = 256
MEM_HEADS = 4
MEM_HD = D_MODEL // MEM_HEADS
D_FF = int(math.ceil(8 * D_MODEL / 3 / 256) * 256)
EPS = 1e-6
LOG2_E = 1.4426950408889634
N_IN_PIECES = 7
PIECE_W = 512

FF_CHUNK = 256
N_FF_CHUNKS = D_FF // FF_CHUNK
SUBLANES = 8
DIAG_BLOCK = SUBLANES
LEVEL_BLOCKS = (8, 16, 32)

CAST_STEPS = 8
TILE_A = 512
TILE_C = 512
VMEM_LIMIT_BYTES = 60 * 1024 * 1024


def _dot(a, b):
    return jnp.dot(a, b, preferred_element_type=F32)


def _dot_nt(a, b):
    return lax.dot_general(a, b, (((1,), (1,)), ((), ())), preferred_element_type=F32)


def _rms(x, g):
    ms = jnp.mean(x * x, axis=-1, keepdims=True)
    return x * lax.rsqrt(ms + EPS) * g


def _silu(x):
    return x * jax.nn.sigmoid(x)


def _split3(x):
    hi = x.astype(BF16)
    r1 = x - hi.astype(F32)
    mid = r1.astype(BF16)
    r2 = r1 - mid.astype(F32)
    lo = r2.astype(BF16)
    return hi, mid, lo


SIDE_GROUP = 12
K_TILE = 256


def _swiglu_into(h_ref, wg_ref, wu_ref, wd_ref, acc_ref, side_work=None, n_side_steps=0):
    n_slots = N_FF_CHUNKS * 3 * (D_MODEL // K_TILE)
    slot_counter = [0, 0]

    def pull():
        slot_counter[0] += 1
        if slot_counter[0] % SIDE_GROUP:
            return
        due = -(-n_side_steps * slot_counter[0] // n_slots)
        while slot_counter[1] < due:
            next(side_work, None)
            slot_counter[1] += 1

    def dot_ktiles(w_ref, cols):
        out = None
        for kt in range(D_MODEL // K_TILE):
            ks = slice(kt * K_TILE, (kt + 1) * K_TILE)
            part = _dot(h_ref[:, ks], w_ref[ks, cols])
            out = part if out is None else out + part
            pull()
        return out

    for c in range(N_FF_CHUNKS):
        cols = slice(c * FF_CHUNK, (c + 1) * FF_CHUNK)
        g = dot_ktiles(wg_ref, cols)
        u = dot_ktiles(wu_ref, cols)
        a = (_silu(g) * u).astype(BF16)
        for nt in range(D_MODEL // K_TILE):
            ns = slice(nt * K_TILE, (nt + 1) * K_TILE)
            d = _dot(a, wd_ref[cols, ns])
            if c == 0:
                acc_ref[:, ns] = d
            else:
                acc_ref[:, ns] += d
            pull()


def _chunk_masks():
    row = lax.broadcasted_iota(jnp.int32, (CHUNK, CHUNK), 0)
    col = lax.broadcasted_iota(jnp.int32, (CHUNK, CHUNK), 1)
    diag_mask = (row // DIAG_BLOCK == col // DIAG_BLOCK) & (col <= row)
    rk = lax.broadcasted_iota(jnp.int32, (CHUNK, HGRN_DK), 0)
    level_masks, odd_rows = [], []
    for bs in LEVEL_BLOCKS:
        level_masks.append((row // (2 * bs) == col // (2 * bs))
                           & ((row // bs) % 2 == 1) & ((col // bs) % 2 == 0))
        odd_rows.append((rk // bs) % 2 == 1)
    nblk = CHUNK // DIAG_BLOCK
    lane = lax.broadcasted_iota(jnp.int32, (nblk, DIAG_BLOCK, CHUNK), 2)
    blk = lax.broadcasted_iota(jnp.int32, (nblk, DIAG_BLOCK, CHUNK), 0)
    lane_blk = lane - blk * DIAG_BLOCK
    return diag_mask, level_masks, odd_rows, lane_blk


HGRN_UNIT_STEPS = 2 + len(LEVEL_BLOCKS) + DIAG_BLOCK

def _hgrn_unit(c, h, masks, q_s, k_s, b_s, v_s, g_s, gh_ref, y_ref, st_ref):
    diag_mask, level_masks, odd_rows, lane_blk = masks
    rows = slice(c * CHUNK, (c + 1) * CHUNK)
    lanes = slice(h * HGRN_DK, (h + 1) * HGRN_DK)
    vl = slice(h * HGRN_DV, (h + 1) * HGRN_DV)
    q = q_s[rows, lanes]
    kbb = k_s[rows, lanes]
    b2 = b_s[rows, lanes]
    v = v_s[rows, vl]
    vb = v.astype(BF16)
    vbt = v.T.astype(BF16)
    b_last = b2[CHUNK - 1:CHUNK, :]
    qb = (q * jnp.exp2(b2)).astype(BF16)
    kd = jnp.exp2(b_last - kbb).astype(BF16)
    st = st_ref[h]
    o_inter = _dot_nt(qb, st.astype(BF16))
    upd = _dot(vbt, kd)
    yield

    def level_operands(bs, odd):
        npair = CHUNK // (2 * bs)
        r = b2.reshape(npair, 2 * bs, HGRN_DK)[:, bs:bs + 1, :]
        r = jnp.broadcast_to(r, (npair, 2 * bs, HGRN_DK)).reshape(CHUNK, HGRN_DK)
        e = jnp.exp2(jnp.where(odd, b2 - r, r - kbb))
        return (q * e).astype(BF16), e.astype(BF16)

    st_ref[h] = st * jnp.exp2(b_last) + upd
    level_scores = []
    for bs, odd in zip(LEVEL_BLOCKS, odd_rows):
        level_scores.append(_dot_nt(*level_operands(bs, odd)))
        yield
    nblk = CHUNK // DIAG_BLOCK
    q3 = q.reshape(nblk, DIAG_BLOCK, HGRN_DK)
    b3 = b2.reshape(nblk, DIAG_BLOCK, HGRN_DK)
    kbb3 = kbb.reshape(nblk, DIAG_BLOCK, HGRN_DK)
    s3 = jnp.zeros((nblk, DIAG_BLOCK, CHUNK), F32)
    for s in range(DIAG_BLOCK):
        tmp = q3 * jnp.exp2(b3 - kbb3[:, s:s + 1, :])
        red = jnp.sum(tmp, axis=-1, keepdims=True)
        s3 = jnp.where(lane_blk == s, red, s3)
        yield
    scores = jnp.where(diag_mask, s3.reshape(CHUNK, CHUNK), 0.0)
    for lmask, ls in zip(level_masks, level_scores):
        scores = jnp.where(lmask, ls, scores)
    o_intra = _dot(scores.astype(BF16), vb)
    o = o_inter + o_intra
    o = o * lax.rsqrt(jnp.mean(o * o, axis=-1, keepdims=True) + EPS) * gh_ref[:, vl]
    y_ref[rows, vl] = (o * g_s[rows, vl]).astype(y_ref.dtype)
    yield


def _hgrn_pieces(q_s, k_s, b_s, v_s, g_s, gh_ref, y_ref, st_ref):
    masks = _chunk_masks()
    for c in range(q_s.shape[0] // CHUNK):
        units = [_hgrn_unit(c, h, masks, q_s, k_s, b_s, v_s, g_s, gh_ref, y_ref, st_ref)
                 for h in range(HGRN_HEADS)]
        while units:
            for u in list(units):
                try:
                    next(u)
                    yield
                except StopIteration:
                    units.remove(u)


def _cast_weight_rows(i, pairs):
    for w_ref, dst_ref in pairs:
        rb = w_ref.shape[0]
        dst_ref[pl.ds(pl.multiple_of(i * rb, rb), rb), :] = w_ref[...].astype(dst_ref.dtype)


def _weight_block_spec(w, l, steps):
    _, rows, cols = w.shape
    return pl.BlockSpec((None, rows // steps, cols), lambda i: (l, jnp.minimum(i, steps - 1), 0))


def _weight_scratch(w):
    return pltpu.VMEM(w.shape[1:], BF16)


def _inproj_tile(side_work, n_side_steps, x_ref, g1_ref, wg_ref, wu_ref, wd_ref, gm_ref, win_ref, lbp_ref,
                 cw_ref, tri_ref, x1_ref, yc_ref, h_ref, acc_ref, ubuf_ref, q_s, k_s, b_s, v_s, g_s):
    tm = x_ref.shape[0]
    x = x_ref[...]
    h_ref[...] = _rms(x, g1_ref[...]).astype(BF16)
    _swiglu_into(h_ref, wg_ref, wu_ref, wd_ref, acc_ref, side_work, n_side_steps)
    x1 = x + 0.5 * acc_ref[...]
    x1_ref[...] = x1
    h_ref[...] = _rms(x1, gm_ref[...]).astype(BF16)

    def proj(p):
        return _dot(h_ref[...], win_ref[:, p * PIECE_W:(p + 1) * PIECE_W])

    lbp = lbp_ref[...]
    lbe = jnp.exp(lbp - jnp.max(lbp, axis=0, keepdims=True))
    lb = lbe[0:1, :] / jnp.sum(lbe, axis=0, keepdims=True)

    u = proj(5) * proj(6)
    ubuf_ref[SUBLANES:SUBLANES + tm, :] = u
    u1 = ubuf_ref[pl.ds(SUBLANES - 1, tm), :]
    u2 = ubuf_ref[pl.ds(SUBLANES - 2, tm), :]
    cw = cw_ref[...]
    yc_ref[...] = (proj(4) * (cw[0:1, :] * u2 + cw[1:2, :] * u1 + cw[2:3, :] * u)).astype(yc_ref.dtype)
    ubuf_ref[0:SUBLANES, :] = ubuf_ref[tm:tm + SUBLANES, :]

    f = lb + (1.0 - lb) * jax.nn.sigmoid(proj(1))
    log2f = jnp.log(f) * LOG2_E
    hi, mid, lo = _split3(log2f)
    tri3 = tri_ref[...]
    b2 = jnp.concatenate(
        [_dot(tri3, jnp.concatenate([t[c * CHUNK:(c + 1) * CHUNK] for t in (hi, mid, lo)], axis=0))
         for c in range(tm // CHUNK)], axis=0)
    b_s[...] = b2
    k_s[...] = b2 - jnp.log(1.0 - f) * LOG2_E

    q_s[...] = _silu(proj(0)) * (HGRN_DK ** -0.5)
    g_s[...] = _silu(proj(3))
    v_s[...] = proj(2)
    for _ in side_work:
        pass


def _stage_ab_kernel(x_ref, g1_ref, wg_f32, wu_f32, wd_f32, gm_ref, win_f32, lbp_ref, cw_ref,
                     tri_ref, gh_ref,
                     x1_ref, yh_ref, yc_ref,
                     wg_ref, wu_ref, wd_ref, win_ref,
                     h_ref, acc_ref, ubuf_ref, q_s, k_s, b_s, v_s, g_s, st_ref,
                     *, tiles_per_seq, n_tiles):
    i = pl.program_id(0)
    j = i - CAST_STEPS
    staging = (q_s, k_s, b_s, v_s, g_s)

    @pl.when(i < CAST_STEPS)
    def _():
        _cast_weight_rows(i, ((wg_f32, wg_ref), (wu_f32, wu_ref), (wd_f32, wd_ref), (win_f32, win_ref)))

    @pl.when(j == 0)
    def _():
        for ref in staging:
            ref[...] = jnp.zeros(ref.shape, F32)

    @pl.when((j + tiles_per_seq - 1) % tiles_per_seq == 0)
    def _():
        st_ref[...] = jnp.zeros_like(st_ref)

    @pl.when(j % tiles_per_seq == 0)
    def _():
        ubuf_ref[0:SUBLANES, :] = jnp.zeros((SUBLANES, CONV_W), F32)

    @pl.when((j >= 0) & (j < n_tiles))
    def _():
        hgrn = _hgrn_pieces(*staging, gh_ref, yh_ref, st_ref)
        _inproj_tile(hgrn, (TILE_A // CHUNK) * HGRN_HEADS * HGRN_UNIT_STEPS, x_ref, g1_ref, wg_ref, wu_ref, wd_ref, gm_ref, win_ref, lbp_ref,
                     cw_ref, tri_ref, x1_ref, yc_ref, h_ref, acc_ref, ubuf_ref, *staging)

    @pl.when(j == n_tiles)
    def _():
        for _ in _hgrn_pieces(*staging, gh_ref, yh_ref, st_ref):
            pass


def _const_spec(shape):
    nd = len(shape)
    return pl.BlockSpec(shape, lambda *_: (0,) * nd, pipeline_mode=pl.Buffered(1))


def _stage_ab(x2d, g1, wg, wu, wd, gm, win, lbp, cw, tri, gh, *, l, seq):
    t = x2d.shape[0]
    tm = TILE_A
    n_tiles = t // tm
    tile = lambda i: jnp.clip(i - CAST_STEPS, 0, n_tiles - 1)
    cur = lambda w: pl.BlockSpec((tm, w), lambda i: (tile(i), 0))
    prev = lambda w: pl.BlockSpec((tm, w), lambda i: (tile(i - 1), 0))
    wspec = lambda w: _weight_block_spec(w, l, CAST_STEPS)
    cspec = lambda c: _const_spec(c.shape)
    stage = pltpu.VMEM((tm, PIECE_W), F32)
    return pl.pallas_call(
        functools.partial(_stage_ab_kernel, tiles_per_seq=seq // tm, n_tiles=n_tiles),
        grid=(CAST_STEPS + n_tiles + 1,),
        in_specs=[cur(D_MODEL), cspec(g1), wspec(wg), wspec(wu), wspec(wd), cspec(gm), wspec(win),
                  cspec(lbp), cspec(cw), cspec(tri), cspec(gh)],
        out_specs=[cur(D_MODEL), prev(HGRN_W), cur(CONV_W)],
        out_shape=[jax.ShapeDtypeStruct((t, D_MODEL), F32), jax.ShapeDtypeStruct((t, HGRN_W), BF16),
                   jax.ShapeDtypeStruct((t, CONV_W), BF16)],
        scratch_shapes=[_weight_scratch(wg), _weight_scratch(wu), _weight_scratch(wd), _weight_scratch(win),
                        pltpu.VMEM((tm, D_MODEL), BF16), pltpu.VMEM((tm, D_MODEL), F32),
                        pltpu.VMEM((tm + SUBLANES, CONV_W), F32),
                        stage, stage, stage, stage, stage,
                        pltpu.VMEM((HGRN_HEADS, HGRN_DV, HGRN_DK), F32)],
        compiler_params=pltpu.CompilerParams(dimension_semantics=("arbitrary",),
                                             vmem_limit_bytes=VMEM_LIMIT_BYTES),
        name="stage_ab_ffn1_inproj_hgrn2",
    )(x2d, g1, wg, wu, wd, gm, win, lbp, cw, tri, gh)


def _stage_c_kernel(x1_ref, yh_ref, yc_ref, mem_ref, gmem_ref, wout_f32, gx_ref, wq_f32, wo_f32, wkv_f32,
                    g2_ref, wg_f32, wu_f32, wd_f32, gf_ref,
                    out_ref,
                    wout_ref, wq_ref, wo_ref, wkv_ref, wg_ref, wu_ref, wd_ref,
                    km_ref, vm_ref, h_ref, acc_ref, att_ref, *, tiles_per_seq):
    i = pl.program_id(0)
    j = i - CAST_STEPS

    @pl.when(i < CAST_STEPS)
    def _():
        _cast_weight_rows(i, ((wout_f32, wout_ref), (wq_f32, wq_ref), (wo_f32, wo_ref), (wkv_f32, wkv_ref),
                              (wg_f32, wg_ref), (wu_f32, wu_ref), (wd_f32, wd_ref)))

    @pl.when((j >= 0) & (j % tiles_per_seq == 0))
    def _():
        mn = _rms(mem_ref[...], gmem_ref[...]).astype(BF16)
        kv = _dot(mn, wkv_ref[...])
        km_ref[...] = kv[:, :D_MODEL].astype(BF16)
        vm_ref[...] = kv[:, D_MODEL:].astype(BF16)

    @pl.when(j >= 0)
    def _():
        x2 = (x1_ref[...]
              + _dot(yh_ref[...], wout_ref[0:HGRN_W, :])
              + _dot(yc_ref[...], wout_ref[HGRN_W:HGRN_W + CONV_W, :]))
        hq = _rms(x2, gx_ref[...]).astype(BF16)
        qm = _dot(hq, wq_ref[...]) * (MEM_HD ** -0.5)
        head = lambda h: slice(h * MEM_HD, (h + 1) * MEM_HD)
        scores = lambda h: _dot_nt(qm[:, head(h)].astype(BF16), km_ref[:, head(h)])
        sc = scores(0)
        for h in range(MEM_HEADS):
            sc_next = scores(h + 1) if h + 1 < MEM_HEADS else None
            e = jnp.exp(sc - jnp.max(sc, axis=-1, keepdims=True))
            pv = _dot(e.astype(BF16), vm_ref[:, head(h)])
            att_ref[:, head(h)] = (pv / jnp.sum(e, axis=-1, keepdims=True)).astype(BF16)
            sc = sc_next
        x3 = x2 + _dot(att_ref[...], wo_ref[...])
        h_ref[...] = _rms(x3, g2_ref[...]).astype(BF16)
        _swiglu_into(h_ref, wg_ref, wu_ref, wd_ref, acc_ref)
        x4 = x3 + 0.5 * acc_ref[...]
        out_ref[...] = _rms(x4, gf_ref[...])


def _stage_c(x1, yh, yc, mem2d, gmem, wout, gx, wq, wo, wkv, g2, wg, wu, wd, gf, *, l, seq):
    t = x1.shape[0]
    tm = TILE_C
    tiles_per_seq = seq // tm
    tile = lambda i: jnp.maximum(i - CAST_STEPS, 0)
    row = lambda w: pl.BlockSpec((tm, w), lambda i: (tile(i), 0))
    memblk = pl.BlockSpec((MEM_LEN, D_MODEL), lambda i: (tile(i) // tiles_per_seq, 0))
    wspec = lambda w: _weight_block_spec(w, l, CAST_STEPS)
    cspec = lambda c: _const_spec(c.shape)
    weights = (wout, wq, wo, wkv, wg, wu, wd)
    return pl.pallas_call(
        functools.partial(_stage_c_kernel, tiles_per_seq=tiles_per_seq),
        grid=(CAST_STEPS + t // tm,),
        in_specs=[row(D_MODEL), row(HGRN_W), row(CONV_W), memblk, cspec(gmem), wspec(wout), cspec(gx),
                  wspec(wq), wspec(wo), wspec(wkv), cspec(g2), wspec(wg), wspec(wu), wspec(wd), cspec(gf)],
        out_specs=row(D_MODEL),
        out_shape=jax.ShapeDtypeStruct((t, D_MODEL), F32),
        scratch_shapes=[_weight_scratch(w) for w in weights] + [
            pltpu.VMEM((MEM_LEN, D_MODEL), BF16), pltpu.VMEM((MEM_LEN, D_MODEL), BF16),
            pltpu.VMEM((tm, D_MODEL), BF16), pltpu.VMEM((tm, D_MODEL), F32),
            pltpu.VMEM((tm, D_MODEL), BF16)],
        compiler_params=pltpu.CompilerParams(dimension_semantics=("arbitrary",),
                                             vmem_limit_bytes=VMEM_LIMIT_BYTES),
        name="stage_c_outproj_xattn_ffn2",
    )(x1, yh, yc, mem2d, gmem, wout, gx, wq, wo, wkv, g2, wg, wu, wd, gf)


def _chunk_tri3():
    r = lax.broadcasted_iota(jnp.int32, (CHUNK, 3 * CHUNK), 0)
    c = lax.broadcasted_iota(jnp.int32, (CHUNK, 3 * CHUNK), 1)
    return (c % CHUNK <= r).astype(BF16)


def kernel(x, mem, ffn1_norm, ffn1_gate, ffn1_up, ffn1_down, mix_norm, w_in, lb_param, hgrn_out_norm, conv_w, w_out, xattn_norm, mem_norm, w_q_mem, w_kv_mem, w_o_mem, ffn2_norm, ffn2_gate, ffn2_up, ffn2_down, final_norm):
    batch, seq, _ = x.shape
    depth = ffn1_norm.shape[0]
    assert depth == 1 and seq % TILE_A == 0 and seq % TILE_C == 0
    l = 0
    x2d = x.reshape(batch * seq, D_MODEL)
    vec = lambda a: a.reshape(1, -1)

    x1, yh, yc = _stage_ab(
        x2d, vec(ffn1_norm[l]), ffn1_gate, ffn1_up, ffn1_down, vec(mix_norm[l]), w_in,
        lb_param, conv_w[l].T, _chunk_tri3(), vec(hgrn_out_norm[l]), l=l, seq=seq)

    out = _stage_c(x1, yh, yc, mem.reshape(batch * MEM_LEN, D_MODEL), vec(mem_norm[l]), w_out,
                   vec(xattn_norm[l]), w_q_mem, w_o_mem, w_kv_mem, vec(ffn2_norm[l]),
                   ffn2_gate, ffn2_up, ffn2_down, vec(final_norm), l=l, seq=seq)
    return out.reshape(batch, seq, D_MODEL)
```

```python
import functools
import math

import jax
import jax.numpy as jnp
from jax import lax
from jax.experimental import pallas as pl
from jax.experimental.pallas import tpu as pltpu

F32 = jnp.float32
BF16 = jnp.bfloat16

D_MODEL = 1024
HGRN_W = 512
CONV_W = 512
HGRN_HEADS = 4
HGRN_DK = 128
HGRN_DV = 128
HGRN_F = HGRN_HEADS * HGRN_DK
CONV_K = 3
CHUNK = 64
MEM_LEN = 256
MEM_HEADS = 4
MEM_HD = D_MODEL // MEM_HEADS
D_FF = int(math.ceil(8 * D_MODEL / 3 / 256) * 256)
EPS = 1e-6
LOG2_E = 1.4426950408889634
N_IN_PIECES = 7
PIECE_W = 512

FF_CHUNK = 256
N_FF_CHUNKS = D_FF // FF_CHUNK
SUBLANES = 8
DIAG_BLOCK = SUBLANES
LEVEL_BLOCKS = (8, 16, 32)

CAST_STEPS = 8
TILE_A = 512
TILE_C = 512
VMEM_LIMIT_BYTES = 60 * 1024 * 1024


def _dot(a, b):
    return jnp.dot(a, b, preferred_element_type=F32)


def _dot_nt(a, b):
    return lax.dot_general(a, b, (((1,), (1,)), ((), ())), preferred_element_type=F32)


def _rms(x, g):
    ms = jnp.mean(x * x, axis=-1, keepdims=True)
    return x * lax.rsqrt(ms + EPS) * g


def _silu(x):
    return x * jax.nn.sigmoid(x)


def _split3(x):
    hi = x.astype(BF16)
    r1 = x - hi.astype(F32)
    mid = r1.astype(BF16)
    r2 = r1 - mid.astype(F32)
    lo = r2.astype(BF16)
    return hi, mid, lo


SIDE_GROUP = 12
K_TILE = 256


def _swiglu_into(h_ref, wg_ref, wu_ref, wd_ref, acc_ref, side_work=None, n_side_steps=0):
    n_slots = N_FF_CHUNKS * 3 * (D_MODEL // K_TILE)
    slot_counter = [0, 0]

    def pull():
        slot_counter[0] += 1
        if slot_counter[0] % SIDE_GROUP:
            return
        due = -(-n_side_steps * slot_counter[0] // n_slots)
        while slot_counter[1] < due:
            next(side_work, None)
            slot_counter[1] += 1

    def dot_ktiles(w_ref, cols):
        out = None
        for kt in range(D_MODEL // K_TILE):
            ks = slice(kt * K_TILE, (kt + 1) * K_TILE)
            part = _dot(h_ref[:, ks], w_ref[ks, cols])
            out = part if out is None else out + part
            pull()
        return out

    for c in range(N_FF_CHUNKS):
        cols = slice(c * FF_CHUNK, (c + 1) * FF_CHUNK)
        g = dot_ktiles(wg_ref, cols)
        u = dot_ktiles(wu_ref, cols)
        a = (_silu(g) * u).astype(BF16)
        for nt in range(D_MODEL // K_TILE):
            ns = slice(nt * K_TILE, (nt + 1) * K_TILE)
            d = _dot(a, wd_ref[cols, ns])
            if c == 0:
                acc_ref[:, ns] = d
            else:
                acc_ref[:, ns] += d
            pull()


def _chunk_masks():
    row = lax.broadcasted_iota(jnp.int32, (CHUNK, CHUNK), 0)
    col = lax.broadcasted_iota(jnp.int32, (CHUNK, CHUNK), 1)
    diag_mask = (row // DIAG_BLOCK == col // DIAG_BLOCK) & (col <= row)
    rk = lax.broadcasted_iota(jnp.int32, (CHUNK, HGRN_DK), 0)
    level_masks, odd_rows = [], []
    for bs in LEVEL_BLOCKS:
        level_masks.append((row // (2 * bs) == col // (2 * bs))
                           & ((row // bs) % 2 == 1) & ((col // bs) % 2 == 0))
        odd_rows.append((rk // bs) % 2 == 1)
    nblk = CHUNK // DIAG_BLOCK
    lane = lax.broadcasted_iota(jnp.int32, (nblk, DIAG_BLOCK, CHUNK), 2)
    blk = lax.broadcasted_iota(jnp.int32, (nblk, DIAG_BLOCK, CHUNK), 0)
    lane_blk = lane - blk * DIAG_BLOCK
    return diag_mask, level_masks, odd_rows, lane_blk


HGRN_UNIT_STEPS = 2 + len(LEVEL_BLOCKS) + DIAG_BLOCK

def _hgrn_unit(c, h, masks, q_s, k_s, b_s, v_s, g_s, gh_ref, y_ref, st_ref):
    diag_mask, level_masks, odd_rows, lane_blk = masks
    rows = slice(c * CHUNK, (c + 1) * CHUNK)
    lanes = slice(h * HGRN_DK, (h + 1) * HGRN_DK)
    vl = slice(h * HGRN_DV, (h + 1) * HGRN_DV)
    q = q_s[rows, lanes]
    kbb = k_s[rows, lanes]
    b2 = b_s[rows, lanes]
    v = v_s[rows, vl]
    vb = v.astype(BF16)
    vbt = v.T.astype(BF16)
    b_last = b2[CHUNK - 1:CHUNK, :]
    qb = (q * jnp.exp2(b2)).astype(BF16)
    kd = jnp.exp2(b_last - kbb).astype(BF16)
    st = st_ref[h]
    o_inter = _dot_nt(qb, st.astype(BF16))
    upd = _dot(vbt, kd)
    yield

    def level_operands(bs, odd):
        npair = CHUNK // (2 * bs)
        r = b2.reshape(npair, 2 * bs, HGRN_DK)[:, bs:bs + 1, :]
        r = jnp.broadcast_to(r, (npair, 2 * bs, HGRN_DK)).reshape(CHUNK, HGRN_DK)
        e = jnp.exp2(jnp.where(odd, b2 - r, r - kbb))
        return (q * e).astype(BF16), e.astype(BF16)

    st_ref[h] = st * jnp.exp2(b_last) + upd
    level_scores = []
    for bs, odd in zip(LEVEL_BLOCKS, odd_rows):
        level_scores.append(_dot_nt(*level_operands(bs, odd)))
        yield
    nblk = CHUNK // DIAG_BLOCK
    q3 = q.reshape(nblk, DIAG_BLOCK, HGRN_DK)
    b3 = b2.reshape(nblk, DIAG_BLOCK, HGRN_DK)
    kbb3 = kbb.reshape(nblk, DIAG_BLOCK, HGRN_DK)
    s3 = jnp.zeros((nblk, DIAG_BLOCK, CHUNK), F32)
    for s in range(DIAG_BLOCK):
        tmp = q3 * jnp.exp2(b3 - kbb3[:, s:s + 1, :])
        red = jnp.sum(tmp, axis=-1, keepdims=True)
        s3 = jnp.where(lane_blk == s, red, s3)
        yield
    scores = jnp.where(diag_mask, s3.reshape(CHUNK, CHUNK), 0.0)
    for lmask, ls in zip(level_masks, level_scores):
        scores = jnp.where(lmask, ls, scores)
    o_intra = _dot(scores.astype(BF16), vb)
    o = o_inter + o_intra
    o = o * lax.rsqrt(jnp.mean(o * o, axis=-1, keepdims=True) + EPS) * gh_ref[:, vl]
    y_ref[rows, vl] = (o * g_s[rows, vl]).astype(y_ref.dtype)
    yield


def _hgrn_pieces(q_s, k_s, b_s, v_s, g_s, gh_ref, y_ref, st_ref):
    masks = _chunk_masks()
    for c in range(q_s.shape[0] // CHUNK):
        units = [_hgrn_unit(c, h, masks, q_s, k_s, b_s, v_s, g_s, gh_ref, y_ref, st_ref)
                 for h in range(HGRN_HEADS)]
        while units:
            for u in list(units):
                try:
                    next(u)
                    yield
                except StopIteration:
                    units.remove(u)


HALF_STEP = 0.5
XATTN_SCALE = MEM_HD ** -0.5
assert math.frexp(XATTN_SCALE)[0] == 0.5


def _cast_weight_rows(i, pairs):
    for w_ref, dst_ref, *scale in pairs:
        rb = w_ref.shape[0]
        w = w_ref[...] * scale[0] if scale else w_ref[...]
        dst_ref[pl.ds(pl.multiple_of(i * rb, rb), rb), :] = w.astype(dst_ref.dtype)


def _weight_block_spec(w, l, steps):
    _, rows, cols = w.shape
    return pl.BlockSpec((None, rows // steps, cols), lambda i: (l, jnp.minimum(i, steps - 1), 0))


def _weight_scratch(w):
    return pltpu.VMEM(w.shape[1:], BF16)


def _inproj_tile(side_work, n_side_steps, x_ref, g1_ref, wg_ref, wu_ref, wd_ref, gm_ref, win_ref, lbp_ref,
                 cw_ref, tri_ref, x1_ref, yc_ref, h_ref, acc_ref, ubuf_ref, q_s, k_s, b_s, v_s, g_s):
    tm = x_ref.shape[0]
    x = x_ref[...]
    h_ref[...] = _rms(x, g1_ref[...]).astype(BF16)
    _swiglu_into(h_ref, wg_ref, wu_ref, wd_ref, acc_ref, side_work, n_side_steps)
    x1 = x + acc_ref[...]
    x1_ref[...] = x1
    h_ref[...] = _rms(x1, gm_ref[...]).astype(BF16)

    def proj(p):
        return _dot(h_ref[...], win_ref[:, p * PIECE_W:(p + 1) * PIECE_W])

    lbp = lbp_ref[...]
    lbe = jnp.exp(lbp - jnp.max(lbp, axis=0, keepdims=True))
    lb = lbe[0:1, :] / jnp.sum(lbe, axis=0, keepdims=True)

    u = proj(5) * proj(6)
    ubuf_ref[SUBLANES:SUBLANES + tm, :] = u
    u1 = ubuf_ref[pl.ds(SUBLANES - 1, tm), :]
    u2 = ubuf_ref[pl.ds(SUBLANES - 2, tm), :]
    cw = cw_ref[...]
    yc_ref[...] = (proj(4) * (cw[0:1, :] * u2 + cw[1:2, :] * u1 + cw[2:3, :] * u)).astype(yc_ref.dtype)
    ubuf_ref[0:SUBLANES, :] = ubuf_ref[tm:tm + SUBLANES, :]

    f = lb + (1.0 - lb) * jax.nn.sigmoid(proj(1))
    log2f = jnp.log(f) * LOG2_E
    hi, mid, lo = _split3(log2f)
    tri3 = tri_ref[...]
    b2 = jnp.concatenate(
        [_dot(tri3, jnp.concatenate([t[c * CHUNK:(c + 1) * CHUNK] for t in (hi, mid, lo)], axis=0))
         for c in range(tm // CHUNK)], axis=0)
    b_s[...] = b2
    k_s[...] = b2 - jnp.log(1.0 - f) * LOG2_E

    q_s[...] = _silu(proj(0)) * (HGRN_DK ** -0.5)
    g_s[...] = _silu(proj(3))
    v_s[...] = proj(2)
    for _ in side_work:
        pass


def _stage_ab_kernel(x_ref, g1_ref, wg_f32, wu_f32, wd_f32, gm_ref, win_f32, lbp_ref, cw_ref,
                     tri_ref, gh_ref,
                     x1_ref, yh_ref, yc_ref,
                     wg_ref, wu_ref, wd_ref, win_ref,
                     h_ref, acc_ref, ubuf_ref, q_s, k_s, b_s, v_s, g_s, st_ref,
                     *, tiles_per_seq, n_tiles):
    i = pl.program_id(0)
    j = i - CAST_STEPS
    staging = (q_s, k_s, b_s, v_s, g_s)

    @pl.when(i < CAST_STEPS)
    def _():
        _cast_weight_rows(i, ((wg_f32, wg_ref), (wu_f32, wu_ref), (wd_f32, wd_ref, HALF_STEP),
                              (win_f32, win_ref)))

    @pl.when(j == 0)
    def _():
        for ref in staging:
            ref[...] = jnp.zeros(ref.shape, F32)

    @pl.when((j + tiles_per_seq - 1) % tiles_per_seq == 0)
    def _():
        st_ref[...] = jnp.zeros_like(st_ref)

    @pl.when(j % tiles_per_seq == 0)
    def _():
        ubuf_ref[0:SUBLANES, :] = jnp.zeros((SUBLANES, CONV_W), F32)

    @pl.when((j >= 0) & (j < n_tiles))
    def _():
        hgrn = _hgrn_pieces(*staging, gh_ref, yh_ref, st_ref)
        _inproj_tile(hgrn, (TILE_A // CHUNK) * HGRN_HEADS * HGRN_UNIT_STEPS, x_ref, g1_ref, wg_ref, wu_ref, wd_ref, gm_ref, win_ref, lbp_ref,
                     cw_ref, tri_ref, x1_ref, yc_ref, h_ref, acc_ref, ubuf_ref, *staging)

    @pl.when(j == n_tiles)
    def _():
        for _ in _hgrn_pieces(*staging, gh_ref, yh_ref, st_ref):
            pass


def _const_spec(shape):
    nd = len(shape)
    return pl.BlockSpec(shape, lambda *_: (0,) * nd, pipeline_mode=pl.Buffered(1))


def _stage_ab(x2d, g1, wg, wu, wd, gm, win, lbp, cw, tri, gh, *, l, seq):
    t = x2d.shape[0]
    tm = TILE_A
    n_tiles = t // tm
    tile = lambda i: jnp.clip(i - CAST_STEPS, 0, n_tiles - 1)
    cur = lambda w: pl.BlockSpec((tm, w), lambda i: (tile(i), 0))
    prev = lambda w: pl.BlockSpec((tm, w), lambda i: (tile(i - 1), 0))
    wspec = lambda w: _weight_block_spec(w, l, CAST_STEPS)
    cspec = lambda c: _const_spec(c.shape)
    stage = pltpu.VMEM((tm, PIECE_W), F32)
    return pl.pallas_call(
        functools.partial(_stage_ab_kernel, tiles_per_seq=seq // tm, n_tiles=n_tiles),
        grid=(CAST_STEPS + n_tiles + 1,),
        in_specs=[cur(D_MODEL), cspec(g1), wspec(wg), wspec(wu), wspec(wd), cspec(gm), wspec(win),
                  cspec(lbp), cspec(cw), cspec(tri), cspec(gh)],
        out_specs=[cur(D_MODEL), prev(HGRN_W), cur(CONV_W)],
        out_shape=[jax.ShapeDtypeStruct((t, D_MODEL), F32), jax.ShapeDtypeStruct((t, HGRN_W), BF16),
                   jax.ShapeDtypeStruct((t, CONV_W), BF16)],
        scratch_shapes=[_weight_scratch(wg), _weight_scratch(wu), _weight_scratch(wd), _weight_scratch(win),
                        pltpu.VMEM((tm, D_MODEL), BF16), pltpu.VMEM((tm, D_MODEL), F32),
                        pltpu.VMEM((tm + SUBLANES, CONV_W), F32),
                        stage, stage, stage, stage, stage,
                        pltpu.VMEM((HGRN_HEADS, HGRN_DV, HGRN_DK), F32)],
        compiler_params=pltpu.CompilerParams(dimension_semantics=("arbitrary",),
                                             vmem_limit_bytes=VMEM_LIMIT_BYTES),
        name="stage_ab_ffn1_inproj_hgrn2",
    )(x2d, g1, wg, wu, wd, gm, win, lbp, cw, tri, gh)


def _stage_c_kernel(x1_ref, yh_ref, yc_ref, mem_ref, gmem_ref, wout_f32, gx_ref, wq_f32, wo_f32, wkv_f32,
                    g2_ref, wg_f32, wu_f32, wd_f32, gf_ref,
                    out_ref,
                    wout_ref, wq_ref, wo_ref, wkv_ref, wg_ref, wu_ref, wd_ref,
                    km_ref, vm_ref, h_ref, acc_ref, att_ref, *, tiles_per_seq):
    i = pl.program_id(0)
    j = i - CAST_STEPS

    @pl.when(i < CAST_STEPS)
    def _():
        _cast_weight_rows(i, ((wout_f32, wout_ref), (wq_f32, wq_ref, XATTN_SCALE), (wo_f32, wo_ref),
                              (wkv_f32, wkv_ref), (wg_f32, wg_ref), (wu_f32, wu_ref),
                              (wd_f32, wd_ref, HALF_STEP)))

    @pl.when((j >= 0) & (j % tiles_per_seq == 0))
    def _():
        mn = _rms(mem_ref[...], gmem_ref[...]).astype(BF16)
        kv = _dot(mn, wkv_ref[...])
        km_ref[...] = kv[:, :D_MODEL].astype(BF16)
        vm_ref[...] = kv[:, D_MODEL:].astype(BF16)

    @pl.when(j >= 0)
    def _():
        x2 = (x1_ref[...]
              + _dot(yh_ref[...], wout_ref[0:HGRN_W, :])
              + _dot(yc_ref[...], wout_ref[HGRN_W:HGRN_W + CONV_W, :]))
        hq = _rms(x2, gx_ref[...]).astype(BF16)
        qm = _dot(hq, wq_ref[...])
        head = lambda h: slice(h * MEM_HD, (h + 1) * MEM_HD)
        scores = lambda h: _dot_nt(qm[:, head(h)].astype(BF16), km_ref[:, head(h)])
        sc = scores(0)
        for h in range(MEM_HEADS):
            sc_next = scores(h + 1) if h + 1 < MEM_HEADS else None
            e = jnp.exp(sc - jnp.max(sc, axis=-1, keepdims=True))
            pv = _dot(e.astype(BF16), vm_ref[:, head(h)])
            att_ref[:, head(h)] = (pv / jnp.sum(e, axis=-1, keepdims=True)).astype(BF16)
            sc = sc_next
        x3 = x2 + _dot(att_ref[...], wo_ref[...])
        h_ref[...] = _rms(x3, g2_ref[...]).astype(BF16)
        _swiglu_into(h_ref, wg_ref, wu_ref, wd_ref, acc_ref)
        x4 = x3 + acc_ref[...]
        out_ref[...] = _rms(x4, gf_ref[...])


def _stage_c(x1, yh, yc, mem2d, gmem, wout, gx, wq, wo, wkv, g2, wg, wu, wd, gf, *, l, seq):
    t = x1.shape[0]
    tm = TILE_C
    tiles_per_seq = seq // tm
    tile = lambda i: jnp.maximum(i - CAST_STEPS, 0)
    row = lambda w: pl.BlockSpec((tm, w), lambda i: (tile(i), 0))
    memblk = pl.BlockSpec((MEM_LEN, D_MODEL), lambda i: (tile(i) // tiles_per_seq, 0))
    wspec = lambda w: _weight_block_spec(w, l, CAST_STEPS)
    cspec = lambda c: _const_spec(c.shape)
    weights = (wout, wq, wo, wkv, wg, wu, wd)
    return pl.pallas_call(
        functools.partial(_stage_c_kernel, tiles_per_seq=tiles_per_seq),
        grid=(CAST_STEPS + t // tm,),
        in_specs=[row(D_MODEL), row(HGRN_W), row(CONV_W), memblk, cspec(gmem), wspec(wout), cspec(gx),
                  wspec(wq), wspec(wo), wspec(wkv), cspec(g2), wspec(wg), wspec(wu), wspec(wd), cspec(gf)],
        out_specs=row(D_MODEL),
        out_shape=jax.ShapeDtypeStruct((t, D_MODEL), F32),
        scratch_shapes=[_weight_scratch(w) for w in weights] + [
            pltpu.VMEM((MEM_LEN, D_MODEL), BF16), pltpu.VMEM((MEM_LEN, D_MODEL), BF16),
            pltpu.VMEM((tm, D_MODEL), BF16), pltpu.VMEM((tm, D_MODEL), F32),
            pltpu.VMEM((tm, D_MODEL), BF16)],
        compiler_params=pltpu.CompilerParams(dimension_semantics=("arbitrary",),
                                             vmem_limit_bytes=VMEM_LIMIT_BYTES),
        name="stage_c_outproj_xattn_ffn2",
    )(x1, yh, yc, mem2d, gmem, wout, gx, wq, wo, wkv, g2, wg, wu, wd, gf)


def _chunk_tri3():
    r = lax.broadcasted_iota(jnp.int32, (CHUNK, 3 * CHUNK), 0)
    c = lax.broadcasted_iota(jnp.int32, (CHUNK, 3 * CHUNK), 1)
    return (c % CHUNK <= r).astype(BF16)


def kernel(x, mem, ffn1_norm, ffn1_gate, ffn1_up, ffn1_down, mix_norm, w_in, lb_param, hgrn_out_norm, conv_w, w_out, xattn_norm, mem_norm, w_q_mem, w_kv_mem, w_o_mem, ffn2_norm, ffn2_gate, ffn2_up, ffn2_down, final_norm):
    batch, seq, _ = x.shape
    depth = ffn1_norm.shape[0]
    assert depth == 1 and seq % TILE_A == 0 and seq % TILE_C == 0
    l = 0
    x2d = x.reshape(batch * seq, D_MODEL)
    vec = lambda a: a.reshape(1, -1)

    x1, yh, yc = _stage_ab(
        x2d, vec(ffn1_norm[l]), ffn1_gate, ffn1_up, ffn1_down, vec(mix_norm[l]), w_in,
        lb_param, conv_w[l].T, _chunk_tri3(), vec(hgrn_out_norm[l]), l=l, seq=seq)

    out = _stage_c(x1, yh, yc, mem.reshape(batch * MEM_LEN, D_MODEL), vec(mem_norm[l]), w_out,
                   vec(xattn_norm[l]), w_q_mem, w_o_mem, w_kv_mem, vec(ffn2_norm[l]),
                   ffn2_gate, ffn2_up, ffn2_down, vec(final_norm), l=l, seq=seq)
    return out.reshape(batch, seq, D_MODEL)
```
